```python
import math
import jax, jax.numpy as jnp
from jax import lax
import numpy as np

D_MODEL = 1024
BATCH = 8
SEQ = 4096
DEPTH = 4

SSD_INNER = D_MODEL
SSD_HEAD_DIM = 64
SSD_HEADS = SSD_INNER // SSD_HEAD_DIM
SSD_GROUPS = 4
SSD_STATE = 128
SSD_CHUNK = 128
SSD_CONV = 4
SSD_BC = SSD_GROUPS * SSD_STATE
SSD_CONV_CH = SSD_INNER + 2 * SSD_BC
DT_MIN = 0.001
DT_MAX = 0.1
ATT_HEAD_DIM = 64
ATT_HEADS = D_MODEL // 128
ATT_INNER = ATT_HEADS * ATT_HEAD_DIM
Q_BLOCK = 128
N_GROUPS = 4
EXPERTS_PER_GROUP = 4
N_EXPERTS = N_GROUPS * EXPERTS_PER_GROUP
TOP_K = 2
EXPERT_FF = D_MODEL // 2
IN_SIZES = (SSD_INNER, SSD_CONV_CH, SSD_HEADS, ATT_INNER, ATT_INNER, ATT_INNER, ATT_HEADS, D_MODEL, D_MODEL)
N_IN = sum(IN_SIZES)
N_MOD = 6
DEEPNORM_ALPHA = (2 * DEPTH) ** 0.25
DEEPNORM_BETA = (8 * DEPTH) ** -0.25
LN_EPS = 1e-5

kernel_name = 'hybrid_ssd_fox_hmoe_deepnorm_adaln'


def layer_norm(x, gain=None, bias=None):
    xf = x.astype(jnp.float32)
    mu = jnp.mean(xf, axis=-1, keepdims=True)
    var = jnp.mean(jnp.square(xf - mu), axis=-1, keepdims=True)
    y = ((xf - mu) * lax.rsqrt(var + LN_EPS)).astype(x.dtype)
    if gain is None:
        return y
    return y * gain + bias


def gated_rms_norm(y, z, w):
    g = (y * jax.nn.silu(z)).astype(jnp.float32)
    g = g * lax.rsqrt(jnp.mean(jnp.square(g), axis=-1, keepdims=True) + LN_EPS)
    return g.astype(y.dtype) * w


def causal_depthwise_conv(u, w, bias):
    k_width, ch = w.shape
    out = lax.conv_general_dilated(
        u, w[:, None, :].astype(u.dtype), window_strides=(1,),
        padding=((k_width - 1, 0),), dimension_numbers=('NWC', 'WIO', 'NWC'),
        feature_group_count=ch)
    return out + bias


def ssd_chunked(xh, dt, a_log, bmat, cmat):
    b, s, h, p = xh.shape
    g, n = bmat.shape[-2:]
    r = h // g
    L = SSD_CHUNK
    nc = s // L
    f32 = jnp.float32
    A = -jnp.exp(a_log.astype(f32))
    a = (dt * A).reshape(b, nc, L, g, r)
    xdt = (xh.astype(f32) * dt[..., None]).reshape(b, nc, L, g, r, p)
    bc = bmat.astype(f32).reshape(b, nc, L, g, n)
    cc = cmat.astype(f32).reshape(b, nc, L, g, n)
    a_cs = jnp.cumsum(a, axis=2)
    causal = jnp.tril(jnp.ones((L, L), dtype=bool))[None, None, :, :, None, None]
    seg = a_cs[:, :, :, None] - a_cs[:, :, None, :]
    decay = jnp.exp(jnp.where(causal, seg, -jnp.inf))
    cb = jnp.einsum('bclgn,bcsgn->bclsg', cc, bc)
    y_diag = jnp.einsum('bclsgr,bcsgrp->bclgrp', cb[..., None] * decay, xdt)
    to_end = jnp.exp(a_cs[:, :, -1:] - a_cs)
    states = jnp.einsum('bclgn,bclgrp->bcgrpn', bc, xdt * to_end[..., None])
    chunk_decay = jnp.exp(a_cs[:, :, -1])

    def carry_state(state, inp):
        st, dec = inp
        return state * dec[..., None, None] + st, state

    _, prev = lax.scan(carry_state, jnp.zeros_like(states[:, 0]),
                       (jnp.moveaxis(states, 1, 0), jnp.moveaxis(chunk_decay, 1, 0)))
    prev = jnp.moveaxis(prev, 0, 1)
    y_off = jnp.einsum('bclgn,bcgrpn->bclgrp', cc, prev) * jnp.exp(a_cs)[..., None]
    return (y_diag + y_off).reshape(b, s, h, p)


def forgetting_attention(q, k, v, log_f):
    b, s, h, d = q.shape
    fcum = jnp.cumsum(log_f.astype(jnp.float32), axis=1).transpose(0, 2, 1)
    scale = d ** -0.5
    q_pos = jnp.arange(Q_BLOCK)
    outs = []
    for i in range(s // Q_BLOCK):
        lo, hi = i * Q_BLOCK, (i + 1) * Q_BLOCK
        logits = jnp.einsum('bqhd,bkhd->bhqk', q[:, lo:hi], k[:, :hi]).astype(jnp.float32) * scale
        logits = logits + fcum[:, :, lo:hi, None] - fcum[:, :, None, :hi]
        causal = jnp.arange(hi)[None, :] <= (lo + q_pos)[:, None]
        logits = jnp.where(causal, logits, -jnp.inf)
        probs = jax.nn.softmax(logits, axis=-1).astype(v.dtype)
        outs.append(jnp.einsum('bhqk,bkhd->bqhd', probs, v[:, :hi]))
    return jnp.concatenate(outs, axis=1)


def hybrid_mixer(h, w_in, conv_w, conv_b, dt_bias, a_log, d_skip, ssd_norm_w, forget_b,
                 w_ssd_o, w_att_o, w_o):
    b, s, _ = h.shape
    split_points = np.cumsum(IN_SIZES)[:-1].tolist()
    proj = h @ w_in
    z, xbc, dt_raw, q, k, v, f_raw, g_ssd, g_att = jnp.split(proj, split_points, axis=-1)
    xbc = jax.nn.silu(causal_depthwise_conv(xbc, conv_w, conv_b))
    xs, bm, cm = jnp.split(xbc, [SSD_INNER, SSD_INNER + SSD_BC], axis=-1)
    dt = jax.nn.softplus((dt_raw + dt_bias).astype(jnp.float32))
    xh = xs.reshape(b, s, SSD_HEADS, SSD_HEAD_DIM)
    y = ssd_chunked(xh, dt, a_log,
                    bm.reshape(b, s, SSD_GROUPS, SSD_STATE),
                    cm.reshape(b, s, SSD_GROUPS, SSD_STATE))
    y = (y + d_skip[:, None] * xh).astype(h.dtype).reshape(b, s, SSD_INNER)
    y_ssd = gated_rms_norm(y, z, ssd_norm_w) @ w_ssd_o
    log_f = jax.nn.log_sigmoid((f_raw + forget_b).astype(jnp.float32))
    o = forgetting_attention(q.reshape(b, s, ATT_HEADS, ATT_HEAD_DIM),
                             k.reshape(b, s, ATT_HEADS, ATT_HEAD_DIM),
                             v.reshape(b, s, ATT_HEADS, ATT_HEAD_DIM), log_f)
    y_att = o.reshape(b, s, ATT_INNER) @ w_att_o
    merged = jax.nn.sigmoid(g_ssd) * y_ssd + jax.nn.sigmoid(g_att) * y_att
    return merged @ w_o


def hierarchical_moe(h, w_rg, b_rg, w_re, b_re, w_gate, w_up, w_down):
    b, s, d = h.shape
    t = h.reshape(-1, d)
    f32 = jnp.float32
    g_prob = jax.nn.softmax((t @ w_rg + b_rg).astype(f32), axis=-1)
    g_p, g_idx = lax.top_k(g_prob, 1)
    onehot_g = jax.nn.one_hot(g_idx[:, 0], N_GROUPS, dtype=f32)
    e_logits = (t @ w_re + b_re).astype(f32).reshape(-1, N_GROUPS, EXPERTS_PER_GROUP)
    e_in = jnp.einsum('tg,tge->te', onehot_g, e_logits)
    top_l, top_i = lax.top_k(e_in, TOP_K)
    w_k = jax.nn.softmax(top_l, axis=-1) * g_p
    w_group = jnp.einsum('tk,tke->te', w_k, jax.nn.one_hot(top_i, EXPERTS_PER_GROUP, dtype=f32))
    gates = (onehot_g[:, :, None] * w_group[:, None, :]).reshape(-1, N_EXPERTS).astype(t.dtype)
    out = jnp.zeros_like(t)
    for e in range(N_EXPERTS):
        hid = jax.nn.silu(t @ w_gate[e]) * (t @ w_up[e])
        out = out + gates[:, e:e + 1] * (hid @ w_down[e])
    return out.reshape(b, s, d)


def setup_inputs(seed: int = 0) -> dict:
    key = jax.random.key(seed)
    ks = iter(jax.random.split(key, 32))
    f32 = jnp.float32

    def nrm(shape, scale):
        return jax.random.normal(next(ks), shape, f32) * scale

    offs = np.concatenate([[0], np.cumsum(IN_SIZES)])
    col_scale = np.ones((N_IN,), np.float32)
    col_scale[offs[1]:offs[1] + SSD_INNER] = DEEPNORM_BETA
    col_scale[offs[5]:offs[6]] = DEEPNORM_BETA
    dt0 = jnp.exp(jax.random.uniform(next(ks), (DEPTH, SSD_HEADS), f32)
                  * (math.log(DT_MAX) - math.log(DT_MIN)) + math.log(DT_MIN))
    return {
        'x': nrm((BATCH, SEQ, D_MODEL), 1.0),
        'c': nrm((BATCH, D_MODEL), 1.0),
        'w_mod': nrm((DEPTH, D_MODEL, N_MOD * D_MODEL), 0.5 * D_MODEL ** -0.5),
        'b_mod': nrm((DEPTH, N_MOD * D_MODEL), 0.02),
        'w_in': nrm((DEPTH, D_MODEL, N_IN), D_MODEL ** -0.5) * jnp.asarray(col_scale),
        'conv_w': nrm((DEPTH, SSD_CONV, SSD_CONV_CH), SSD_CONV ** -0.5),
        'conv_b': nrm((DEPTH, SSD_CONV_CH), 0.02),
        'dt_bias': dt0 + jnp.log(-jnp.expm1(-dt0)),
        'a_log': jnp.log(jax.random.uniform(next(ks), (DEPTH, SSD_HEADS), f32, minval=1.0, maxval=16.0)),
        'd_skip': 1.0 + nrm((DEPTH, SSD_HEADS), 0.1),
        'ssd_norm_w': 1.0 + nrm((DEPTH, SSD_INNER), 0.05),
        'forget_b': jax.random.uniform(next(ks), (DEPTH, ATT_HEADS), f32, minval=1.0, maxval=4.0),
        'w_ssd_o': nrm((DEPTH, SSD_INNER, D_MODEL), DEEPNORM_BETA * SSD_INNER ** -0.5),
        'w_att_o': nrm((DEPTH, ATT_INNER, D_MODEL), DEEPNORM_BETA * ATT_INNER ** -0.5),
        'w_o': nrm((DEPTH, D_MODEL, D_MODEL), DEEPNORM_BETA * D_MODEL ** -0.5),
        'ln1_g': 1.0 + nrm((DEPTH, D_MODEL), 0.05),
        'ln1_b': nrm((DEPTH, D_MODEL), 0.02),
        'w_router_group': nrm((DEPTH, D_MODEL, N_GROUPS), D_MODEL ** -0.5),
        'b_router_group': nrm((DEPTH, N_GROUPS), 0.01),
        'w_router_expert': nrm((DEPTH, D_MODEL, N_EXPERTS), D_MODEL ** -0.5),
        'b_router_expert': nrm((DEPTH, N_EXPERTS), 0.01),
        'w_gate': nrm((DEPTH, N_EXPERTS, D_MODEL, EXPERT_FF), D_MODEL ** -0.5),
        'w_up': nrm((DEPTH, N_EXPERTS, D_MODEL, EXPERT_FF), DEEPNORM_BETA * D_MODEL ** -0.5),
        'w_down': nrm((DEPTH, N_EXPERTS, EXPERT_FF, D_MODEL), DEEPNORM_BETA * EXPERT_FF ** -0.5),
        'ln2_g': 1.0 + nrm((DEPTH, D_MODEL), 0.05),
        'ln2_b': nrm((DEPTH, D_MODEL), 0.02),
    }


def reference(x, c, w_mod, b_mod, w_in, conv_w, conv_b, dt_bias, a_log, d_skip, ssd_norm_w,
              forget_b, w_ssd_o, w_att_o, w_o, ln1_g, ln1_b, w_router_group, b_router_group,
              w_router_expert, b_router_expert, w_gate, w_up, w_down, ln2_g, ln2_b):
    c_act = jax.nn.silu(c)
    for l in range(DEPTH):
        mod = (c_act @ w_mod[l] + b_mod[l])[:, None, :]
        sh1, sc1, gt1, sh2, sc2, gt2 = jnp.split(mod, N_MOD, axis=-1)
        h = layer_norm(x) * (1.0 + sc1) + sh1
        y = hybrid_mixer(h, w_in[l], conv_w[l], conv_b[l], dt_bias[l], a_log[l], d_skip[l],
                         ssd_norm_w[l], forget_b[l], w_ssd_o[l], w_att_o[l], w_o[l])
        x = layer_norm(DEEPNORM_ALPHA * x + (1.0 + gt1) * y, ln1_g[l], ln1_b[l])
        h = layer_norm(x) * (1.0 + sc2) + sh2
        y = hierarchical_moe(h, w_router_group[l], b_router_group[l], w_router_expert[l],
                             b_router_expert[l], w_gate[l], w_up[l], w_down[l])
        x = layer_norm(DEEPNORM_ALPHA * x + (1.0 + gt2) * y, ln2_g[l], ln2_b[l])
    return x
```

```python
import functools
import math

import numpy as np
import jax
import jax.numpy as jnp
from jax import lax
from jax.experimental import pallas as pl
from jax.experimental.pallas import tpu as pltpu

SSD_HEAD_DIM = 64
SSD_HEADS = 16
SSD_GROUPS = 4
SSD_STATE = 128
SSD_CHUNK = 128
SSD_CONV = 4
SSD_INNER = SSD_HEADS * SSD_HEAD_DIM
SSD_BC = SSD_GROUPS * SSD_STATE
SSD_CONV_CH = SSD_INNER + 2 * SSD_BC
ATT_HEAD_DIM = 64
ATT_HEADS = 8
ATT_INNER = ATT_HEADS * ATT_HEAD_DIM
N_GROUPS = 4
EXPERTS_PER_GROUP = 4
N_EXPERTS = 16
N_MOD = 6
DEPTH_FOR_DEEPNORM = 4
DEEPNORM_ALPHA = (2 * DEPTH_FOR_DEEPNORM) ** 0.25
LN_EPS = 1e-5

LANES = 128
VMEM_LIMIT = 56 * 1024 * 1024

F32 = jnp.float32
BF16 = jnp.bfloat16
HIGHEST = lax.Precision.HIGHEST


def _sigmoid(x):
    return 1.0 / (1.0 + jnp.exp(-x))


def _softplus(x):
    return jnp.maximum(x, 0.0) + jnp.log(1.0 + jnp.exp(-jnp.abs(x)))


def _layer_norm_rows(x):
    mu = jnp.mean(x, axis=-1, keepdims=True)
    xc = x - mu
    var = jnp.mean(xc * xc, axis=-1, keepdims=True)
    return xc * lax.rsqrt(var + LN_EPS)


def _const_spec(shape):
    n = len(shape)
    return pl.BlockSpec(shape, lambda *_: (0,) * n)


def _mod_kernel(c_ref, w_ref, b_ref, o_ref):
    c = c_ref[...]
    ca = c * _sigmoid(c)
    o_ref[0] = jnp.dot(ca, w_ref[0], precision=HIGHEST, preferred_element_type=F32) + b_ref[0]


def _modulation(c, w_mod, b_mod):
    depth, d, n = w_mod.shape
    bsz = c.shape[0]
    tn = 1024
    return pl.pallas_call(
        _mod_kernel,
        grid=(depth, n // tn),
        in_specs=[
            pl.BlockSpec((bsz, d), lambda l, j: (0, 0)),
            pl.BlockSpec((1, d, tn), lambda l, j: (l, 0, j)),
            pl.BlockSpec((1, 1, tn), lambda l, j: (l, 0, j)),
        ],
        out_specs=pl.BlockSpec((1, bsz, tn), lambda l, j: (l, 0, j)),
        out_shape=jax.ShapeDtypeStruct((depth, bsz, n), F32),
        compiler_params=pltpu.CompilerParams(
            dimension_semantics=("arbitrary", "arbitrary"), vmem_limit_bytes=VMEM_LIMIT),
        name="modulation",
    )(c, w_mod, b_mod.reshape(depth, 1, n))


def _inproj_kernel(x_ref, mod_ref, wz_ref, wxbc_ref, wq_ref, wk_ref, wv_ref, wgs_ref, wga_ref,
                   wsm_ref, z_ref, xbc_ref, q_ref, k_ref, v_ref, gs_ref, ga_ref, sm_ref):
    x = x_ref[...]
    shift = mod_ref[0:1, :]
    scale = mod_ref[1:2, :]
    h = _layer_norm_rows(x) * (1.0 + scale) + shift
    hb = h.astype(BF16)
    for w_ref, o_ref in ((wz_ref, z_ref), (wxbc_ref, xbc_ref), (wq_ref, q_ref), (wk_ref, k_ref),
                         (wv_ref, v_ref), (wgs_ref, gs_ref), (wga_ref, ga_ref)):
        o_ref[...] = jnp.dot(hb, w_ref[...], preferred_element_type=F32).astype(o_ref.dtype)
    sm_ref[...] = jnp.dot(h, wsm_ref[...], precision=HIGHEST, preferred_element_type=F32)


def _inproj(x2, mod_l, weights, w_small, seq, tm):
    t, d = x2.shape
    tiles_per_batch = seq // tm
    widths = [w.shape[1] for w in weights]
    in_specs = [
        pl.BlockSpec((tm, d), lambda i: (i, 0)),
        pl.BlockSpec((None, N_MOD, d), lambda i: (i // tiles_per_batch, 0, 0)),
    ]
    in_specs += [_const_spec(w.shape) for w in weights]
    in_specs += [_const_spec(w_small.shape)]
    out_specs = [pl.BlockSpec((tm, n), lambda i: (i, 0)) for n in widths]
    out_specs += [pl.BlockSpec((tm, LANES), lambda i: (i, 0))]
    out_shape = [jax.ShapeDtypeStruct((t, n), BF16) for n in widths]
    out_shape += [jax.ShapeDtypeStruct((t, LANES), F32)]
    return pl.pallas_call(
        _inproj_kernel,
        grid=(t // tm,),
        in_specs=in_specs,
        out_specs=out_specs,
        out_shape=out_shape,
        compiler_params=pltpu.CompilerParams(
            dimension_semantics=("arbitrary",), vmem_limit_bytes=VMEM_LIMIT),
        name="inproj",
    )(x2, mod_l, *weights, w_small)


def _ssd_constants():
    L = SSD_CHUNK
    sh = np.zeros((3 * L, 2 * L), np.float32)
    for j in range(1, SSD_CONV):
        for t in range(L):
            sh[(j - 1) * L + t, L + t - j] = 1.0
    tril = np.tril(np.ones((L, L), np.float32))
    e3 = np.zeros((LANES, SSD_INNER), np.float32)
    for piece in range(3):
        for h in range(SSD_HEADS):
            e3[32 * piece + h, h * SSD_HEAD_DIM:(h + 1) * SSD_HEAD_DIM] = 1.0
    r = SSD_HEADS // SSD_GROUPS
    bd = np.zeros((r * L, r * SSD_HEAD_DIM), np.float32)
    for i in range(r):
        bd[i * L:(i + 1) * L, i * SSD_HEAD_DIM:(i + 1) * SSD_HEAD_DIM] = 1.0
    return (jnp.asarray(sh, BF16), jnp.asarray(tril, F32), jnp.asarray(e3, BF16),
            jnp.asarray(bd, BF16))


def _expand_heads(xm, e3):
    hi = xm.astype(BF16)
    r1 = xm - hi.astype(F32)
    mid = r1.astype(BF16)
    lo = (r1 - mid.astype(F32)).astype(BF16)
    packed = (hi.astype(F32) + pltpu.roll(mid.astype(F32), 32, 1)
              + pltpu.roll(lo.astype(F32), 64, 1)).astype(BF16)
    return jnp.dot(packed, e3, preferred_element_type=F32)


def _ssd_kernel(xbc_ref, sm_ref, cw_ref, cb_ref, hv_ref, dskip_ref, sh_ref, tril_ref, e3_ref,
                bd_ref, y_ref, fcum_ref, uext, state, fcarry, *, rows):
    L = SSD_CHUNK
    j = pl.program_id(1)

    @pl.when(j == 0)
    def _():
        uext[0:L, :] = jnp.zeros((L, SSD_CONV_CH), BF16)
        state[...] = jnp.zeros_like(state)
        fcarry[...] = jnp.zeros_like(fcarry)

    uext[L:L + rows, :] = xbc_ref[0]

    lane = lax.broadcasted_iota(jnp.int32, (1, LANES), 1)
    is_dt = lane < SSD_HEADS
    is_f = jnp.logical_and(lane >= SSD_HEADS, lane < SSD_HEADS + ATT_HEADS)
    a_neg = -jnp.exp(hv_ref[1:2, :])
    bias = hv_ref[0:1, :]
    rr = lax.broadcasted_iota(jnp.int32, (L, L), 0)
    cc = lax.broadcasted_iota(jnp.int32, (L, L), 1)
    causal = rr >= cc
    gw = (SSD_HEADS // SSD_GROUPS) * SSD_HEAD_DIM

    for c in range(rows // L):
        r0 = c * L
        ue = uext[r0:r0 + 2 * L, :]
        ucur = ue[L:2 * L, :].astype(F32)
        shifted = jnp.dot(sh_ref[...], ue, preferred_element_type=F32)
        conv = (cw_ref[3:4, :] * ucur + cw_ref[2:3, :] * shifted[0:L]
                + cw_ref[1:2, :] * shifted[L:2 * L] + cw_ref[0:1, :] * shifted[2 * L:3 * L]
                + cb_ref[...])
        act = conv * _sigmoid(conv)
        xs = act[:, :SSD_INNER]
        bm = act[:, SSD_INNER:SSD_INNER + SSD_BC]
        cm = act[:, SSD_INNER + SSD_BC:]

        pre = sm_ref[0, r0:r0 + L, :] + bias
        dt = jnp.where(is_dt, _softplus(pre), 0.0)
        log_f = jnp.minimum(pre, 0.0) - jnp.log(1.0 + jnp.exp(-jnp.abs(pre)))
        comb = jnp.where(is_dt, dt * a_neg, jnp.where(is_f, log_f, 0.0))
        cs = jnp.dot(tril_ref[...], comb, precision=HIGHEST, preferred_element_type=F32)
        fc = cs + fcarry[...]
        fcum_ref[0, r0:r0 + L, :] = fc
        fcarry[...] = jnp.where(is_f, fc[L - 1:L, :], 0.0)

        a_cs = jnp.where(is_dt, cs, 0.0)
        a_cs_x = _expand_heads(a_cs, e3_ref[...])
        dt_x = _expand_heads(dt, e3_ref[...])
        a_last_x = a_cs_x[L - 1:L, :]
        xdt = xs * dt_x
        xdt_b = xdt.astype(BF16)
        xend_b = (xdt * jnp.exp(a_last_x - a_cs_x)).astype(BF16)
        e_acs = jnp.exp(a_cs_x)
        e_last = jnp.exp(a_last_x)
        a_cs_t = jnp.transpose(a_cs)

        y_groups = []
        for g in range(SSD_GROUPS):
            bg = bm[:, g * SSD_STATE:(g + 1) * SSD_STATE]
            cg_b = cm[:, g * SSD_STATE:(g + 1) * SSD_STATE].astype(BF16)
            cb = lax.dot_general(cg_b, bg.astype(BF16), (((1,), (1,)), ((), ())),
                                 preferred_element_type=F32)
            ms = []
            for r in range(SSD_HEADS // SSD_GROUPS):
                h = g * (SSD_HEADS // SSD_GROUPS) + r
                seg = a_cs[:, h:h + 1] - a_cs_t[h:h + 1, :]
                dec = jnp.exp(jnp.where(causal, seg, -jnp.inf))
                ms.append((cb * dec).astype(BF16))
            m_cat = jnp.concatenate(ms, axis=1)
            xg = xdt_b[:, g * gw:(g + 1) * gw]
            x_bd = jnp.concatenate([xg] * (SSD_HEADS // SSD_GROUPS), axis=0) * bd_ref[...]
            y_diag = jnp.dot(m_cat, x_bd, preferred_element_type=F32)
            st = state[:, g * gw:(g + 1) * gw]
            y_off = jnp.dot(cg_b, st.astype(BF16), preferred_element_type=F32)
            y_groups.append(y_diag + y_off * e_acs[:, g * gw:(g + 1) * gw])
            new_st = jnp.dot(jnp.transpose(bg).astype(BF16), xend_b[:, g * gw:(g + 1) * gw],
                             preferred_element_type=F32)
            state[:, g * gw:(g + 1) * gw] = st * e_last[:, g * gw:(g + 1) * gw] + new_st
        y = jnp.concatenate(y_groups, axis=1) + dskip_ref[...] * xs
        y_ref[0, r0:r0 + L, :] = y.astype(y_ref.dtype)

    uext[0:L, :] = uext[rows:rows + L, :]


def _ssd(xbc, small, conv_w, conv_b, hvec, dskip_x, consts, rows):
    bsz, seq, _ = xbc.shape
    sh, tril, e3, bd = consts
    kern = functools.partial(_ssd_kernel, rows=rows)
    return pl.pallas_call(
        kern,
        grid=(bsz, seq // rows),
        in_specs=[
            pl.BlockSpec((1, rows, SSD_CONV_CH), lambda b, j: (b, j, 0)),
            pl.BlockSpec((1, rows, LANES), lambda b, j: (b, j, 0)),
            _const_spec(conv_w.shape), _const_spec(conv_b.shape), _const_spec(hvec.shape),
            _const_spec(dskip_x.shape), _const_spec(sh.shape), _const_spec(tril.shape),
            _const_spec(e3.shape), _const_spec(bd.shape),
        ],
        out_specs=[
            pl.BlockSpec((1, rows, SSD_INNER), lambda b, j: (b, j, 0)),
            pl.BlockSpec((1, rows, LANES), lambda b, j: (b, j, 0)),
        ],
        out_shape=[
            jax.ShapeDtypeStruct((bsz, seq, SSD_INNER), BF16),
            jax.ShapeDtypeStruct((bsz, seq, LANES), F32),
        ],
        scratch_shapes=[
            pltpu.VMEM((SSD_CHUNK + rows, SSD_CONV_CH), BF16),
            pltpu.VMEM((SSD_STATE, SSD_INNER), F32),
            pltpu.VMEM((1, LANES), F32),
        ],
        compiler_params=pltpu.CompilerParams(
            dimension_semantics=("arbitrary", "arbitrary"), vmem_limit_bytes=VMEM_LIMIT),
        name="ssd",
    )(xbc, small, conv_w, conv_b, hvec, dskip_x, sh, tril, e3, bd)


def _attn_kernel(q_ref, k_ref, v_ref, fcol_ref, frow_ref, o_ref, m_sc, l_sc, acc_sc, *, tq):
    qi = pl.program_id(2)
    lane = lax.broadcasted_iota(jnp.int32, (1, LANES), 1)
    low = lane < ATT_HEAD_DIM
    q = q_ref[0] * jnp.asarray(ATT_HEAD_DIM ** -0.5, BF16)
    zero = jnp.zeros_like(q)
    qh = (jnp.where(low, q, zero), jnp.where(low, zero, q))
    fq = (fcol_ref[0, 0, :, 0:1], fcol_ref[0, 0, :, 1:2])

    m_sc[...] = jnp.full_like(m_sc, -jnp.inf)
    l_sc[...] = jnp.zeros_like(l_sc)
    acc_sc[...] = jnp.zeros_like(acc_sc)

    rr = lax.broadcasted_iota(jnp.int32, (tq, tq), 0)
    cc = lax.broadcasted_iota(jnp.int32, (tq, tq), 1)
    causal = rr >= cc

    def block(jb, masked):
        start = pl.multiple_of(jb * tq, tq)
        kb = k_ref[0, pl.ds(start, tq), :]
        vb = v_ref[0, pl.ds(start, tq), :]
        for hd in range(2):
            fk = frow_ref[0, 0, hd:hd + 1, pl.ds(start, tq)]
            s = lax.dot_general(qh[hd], kb, (((1,), (1,)), ((), ())),
                                preferred_element_type=F32)
            s = s + (fq[hd] - fk)
            if masked:
                s = jnp.where(causal, s, -jnp.inf)
            m_old = m_sc[hd]
            m_new = jnp.maximum(m_old, jnp.max(s, axis=1, keepdims=True))
            p = jnp.exp(s - m_new)
            alpha = jnp.exp(m_old - m_new)
            l_sc[hd] = alpha * l_sc[hd] + jnp.sum(p, axis=1, keepdims=True)
            acc_sc[hd] = alpha * acc_sc[hd] + jnp.dot(p.astype(BF16), vb,
                                                      preferred_element_type=F32)
            m_sc[hd] = m_new

    def body(jb, carry):
        block(jb, False)
        return carry

    lax.fori_loop(0, qi, body, 0)
    block(qi, True)

    o0 = acc_sc[0] / l_sc[0]
    o1 = acc_sc[1] / l_sc[1]
    o_ref[0] = jnp.where(low, o0, o1).astype(o_ref.dtype)


def _attention(q, k, v, fcol, frow, tq):
    bsz, seq, _ = q.shape
    pairs = ATT_HEADS // 2
    kern = functools.partial(_attn_kernel, tq=tq)
    return pl.pallas_call(
        kern,
        grid=(bsz, pairs, seq // tq),
        in_specs=[
            pl.BlockSpec((1, tq, LANES), lambda b, p, i: (b, i, p)),
            pl.BlockSpec((1, seq, LANES), lambda b, p, i: (b, 0, p)),
            pl.BlockSpec((1, seq, LANES), lambda b, p, i: (b, 0, p)),
            pl.BlockSpec((1, 1, tq, 2), lambda b, p, i: (b, p, i, 0)),
            pl.BlockSpec((1, 1, 2, seq), lambda b, p, i: (b, p, 0, 0)),
        ],
        out_specs=pl.BlockSpec((1, tq, LANES), lambda b, p, i: (b, i, p)),
        out_shape=jax.ShapeDtypeStruct((bsz, seq, ATT_INNER), BF16),
        scratch_shapes=[
            pltpu.VMEM((2, tq, 1), F32),
            pltpu.VMEM((2, tq, 1), F32),
            pltpu.VMEM((2, tq, LANES), F32),
        ],
        compiler_params=pltpu.CompilerParams(
            dimension_semantics=("arbitrary", "arbitrary", "arbitrary"),
            vmem_limit_bytes=VMEM_LIMIT),
        name="fox_attention",
    )(q, k, v, fcol, frow)


def _route(logits):
    lane = lax.broadcasted_iota(jnp.int32, logits.shape, 1)
    neg = -jnp.inf
    big = jnp.int32(1 << 20)
    is_g = jnp.logical_and(lane >= N_EXPERTS, lane < N_EXPERTS + N_GROUPS)
    gl = jnp.where(is_g, logits, neg)
    gmax = jnp.max(gl, axis=1, keepdims=True)
    gsum = jnp.sum(jnp.exp(gl - gmax), axis=1, keepdims=True)
    g_p = 1.0 / gsum
    g_idx = jnp.min(jnp.where(gl == gmax, lane, big), axis=1, keepdims=True) - N_EXPERTS
    lo = g_idx * EXPERTS_PER_GROUP
    in_group = jnp.logical_and(lane >= lo, lane < lo + EXPERTS_PER_GROUP)
    el = jnp.where(in_group, logits, neg)
    m1 = jnp.max(el, axis=1, keepdims=True)
    i1 = jnp.min(jnp.where(el == m1, lane, big), axis=1, keepdims=True)
    el2 = jnp.where(lane == i1, neg, el)
    m2 = jnp.max(el2, axis=1, keepdims=True)
    i2 = jnp.min(jnp.where(el2 == m2, lane, big), axis=1, keepdims=True)
    e2 = jnp.exp(m2 - m1)
    w1 = g_p / (1.0 + e2)
    w2 = g_p * e2 / (1.0 + e2)
    return jnp.where(lane == i1, w1, 0.0) + jnp.where(lane == i2, w2, 0.0)


def _mixout_kernel(y_ref, z_ref, o_ref, gs_ref, ga_ref, x_ref, mod_ref, nw_ref, wso_ref, wao_ref,
                   wo_ref, lng_ref, lnb_ref, wr_ref, br_ref, x1_ref, h2_ref, gates_ref):
    g = y_ref[...].astype(F32)
    z = z_ref[...].astype(F32)
    g = g * (z * _sigmoid(z))
    g = g * lax.rsqrt(jnp.mean(g * g, axis=-1, keepdims=True) + LN_EPS) * nw_ref[...]
    y_ssd = jnp.dot(g.astype(BF16), wso_ref[...], preferred_element_type=F32)
    y_att = jnp.dot(o_ref[...], wao_ref[...], preferred_element_type=F32)
    merged = (_sigmoid(gs_ref[...].astype(F32)) * y_ssd
              + _sigmoid(ga_ref[...].astype(F32)) * y_att)
    mix = jnp.dot(merged.astype(BF16), wo_ref[...], preferred_element_type=F32)
    gate1 = mod_ref[2:3, :]
    x1 = _layer_norm_rows(DEEPNORM_ALPHA * x_ref[...] + (1.0 + gate1) * mix)
    x1 = x1 * lng_ref[...] + lnb_ref[...]
    x1_ref[...] = x1
    h2 = _layer_norm_rows(x1) * (1.0 + mod_ref[4:5, :]) + mod_ref[3:4, :]
    h2_ref[...] = h2.astype(BF16)
    logits = jnp.dot(h2, wr_ref[...], precision=HIGHEST, preferred_element_type=F32) + br_ref[...]
    gates_ref[...] = _route(logits)


def _mixout(y, z, o, gs, ga, x2, mod_l, norm_w, w_ssd_o, w_att_o, w_o, ln_g, ln_b, w_r, b_r,
            seq, tm):
    t, d = x2.shape
    tiles_per_batch = seq // tm
    row = lambda n: pl.BlockSpec((tm, n), lambda i: (i, 0))
    return pl.pallas_call(
        _mixout_kernel,
        grid=(t // tm,),
        in_specs=[
            row(SSD_INNER), row(SSD_INNER), row(ATT_INNER), row(d), row(d), row(d),
            pl.BlockSpec((None, N_MOD, d), lambda i: (i // tiles_per_batch, 0, 0)),
            _const_spec(norm_w.shape), _const_spec(w_ssd_o.shape), _const_spec(w_att_o.shape),
            _const_spec(w_o.shape), _const_spec(ln_g.shape), _const_spec(ln_b.shape),
            _const_spec(w_r.shape), _const_spec(b_r.shape),
        ],
        out_specs=[row(d), row(d), row(LANES)],
        out_shape=[
            jax.ShapeDtypeStruct((t, d), F32),
            jax.ShapeDtypeStruct((t, d), BF16),
            jax.ShapeDtypeStruct((t, LANES), F32),
        ],
        compiler_params=pltpu.CompilerParams(
            dimension_semantics=("arbitrary",), vmem_limit_bytes=VMEM_LIMIT),
        name="mixer_out",
    )(y, z, o, gs, ga, x2, mod_l, norm_w, w_ssd_o, w_att_o, w_o, ln_g, ln_b, w_r, b_r)


def _moe_kernel(h_ref, gates_ref, x1_ref, mod_ref, wg_ref, wu_ref, wd_ref, lng_ref, lnb_ref,
                o_ref, acc_ref):
    e = pl.program_id(1)

    @pl.when(e == 0)
    def _():
        acc_ref[...] = jnp.zeros_like(acc_ref)

    h = h_ref[...]
    a = jnp.dot(h, wg_ref[0], preferred_element_type=F32)
    u = jnp.dot(h, wu_ref[0], preferred_element_type=F32)
    hid = (a * _sigmoid(a)) * u
    lane = lax.broadcasted_iota(jnp.int32, (1, LANES), 1)
    gate = jnp.sum(jnp.where(lane == e, gates_ref[...], 0.0), axis=1, keepdims=True)
    acc_ref[...] += gate * jnp.dot(hid.astype(BF16), wd_ref[0], preferred_element_type=F32)

    @pl.when(e == pl.num_programs(1) - 1)
    def _():
        gate2 = mod_ref[5:6, :]
        xn = _layer_norm_rows(DEEPNORM_ALPHA * x1_ref[...] + (1.0 + gate2) * acc_ref[...])
        o_ref[...] = xn * lng_ref[...] + lnb_ref[...]


def _moe(h2, gates, x1, mod_l, w_gate, w_up, w_down, ln_g, ln_b, seq, tm):
    t, d = x1.shape
    n_e, _, ff = w_gate.shape
    tiles_per_batch = seq // tm
    return pl.pallas_call(
        _moe_kernel,
        grid=(t // tm, n_e),
        in_specs=[
            pl.BlockSpec((tm, d), lambda i, e: (i, 0)),
            pl.BlockSpec((tm, LANES), lambda i, e: (i, 0)),
            pl.BlockSpec((tm, d), lambda i, e: (i, 0)),
            pl.BlockSpec((None, N_MOD, d), lambda i, e: (i // tiles_per_batch, 0, 0)),
            pl.BlockSpec((1, d, ff), lambda i, e: (e, 0, 0)),
            pl.BlockSpec((1, d, ff), lambda i, e: (e, 0, 0)),
            pl.BlockSpec((1, ff, d), lambda i, e: (e, 0, 0)),
            pl.BlockSpec((1, d), lambda i, e: (0, 0)),
            pl.BlockSpec((1, d), lambda i, e: (0, 0)),
        ],
        out_specs=pl.BlockSpec((tm, d), lambda i, e: (i, 0)),
        out_shape=jax.ShapeDtypeStruct((t, d), F32),
        scratch_shapes=[pltpu.VMEM((tm, d), F32)],
        compiler_params=pltpu.CompilerParams(
            dimension_semantics=("arbitrary", "arbitrary"), vmem_limit_bytes=VMEM_LIMIT),
        name="moe",
    )(h2, gates, x1, mod_l, w_gate, w_up, w_down, ln_g, ln_b)


def _pad_lanes(a):
    return jnp.pad(a, ((0, 0), (0, LANES - a.shape[1])))


def kernel(x, c, w_mod, b_mod, w_in, conv_w, conv_b, dt_bias, a_log, d_skip, ssd_norm_w, forget_b,
           w_ssd_o, w_att_o, w_o, ln1_g, ln1_b, w_router_group, b_router_group, w_router_expert,
           b_router_expert, w_gate, w_up, w_down, ln2_g, ln2_b):
    bsz, seq, d = x.shape
    depth = w_mod.shape[0]
    t = bsz * seq
    tm_proj = min(512, seq)
    tm_moe = min(1024, seq)
    ssd_rows = min(512, seq)
    tq = min(512, seq)

    mod = _modulation(c, w_mod, b_mod).reshape(depth, bsz, N_MOD, d)
    consts = _ssd_constants()
    in_sizes = (SSD_INNER, SSD_CONV_CH, SSD_HEADS, ATT_INNER, ATT_INNER, ATT_INNER, ATT_HEADS, d, d)
    offs = np.concatenate([[0], np.cumsum(in_sizes)]).tolist()

    x2 = x.reshape(t, d)
    for l in range(depth):
        cols = [w_in[l][:, offs[i]:offs[i + 1]] for i in range(len(in_sizes))]
        wz, wxbc, wdt, wq, wk, wv, wf, wgs, wga = cols
        big = [w.astype(BF16) for w in (wz, wxbc, wq, wk, wv, wgs, wga)]
        w_small = _pad_lanes(jnp.concatenate([wdt, wf], axis=1))
        z, xbc, q, k, v, gs, ga, small = _inproj(x2, mod[l], big, w_small, seq, tm_proj)

        hvec = jnp.concatenate([
            _pad_lanes(jnp.concatenate([dt_bias[l], forget_b[l]])[None, :]),
            _pad_lanes(a_log[l][None, :]),
            jnp.zeros((6, LANES), F32)], axis=0)
        dskip_x = jnp.repeat(d_skip[l], SSD_HEAD_DIM)[None, :]
        y, fcum = _ssd(xbc.reshape(bsz, seq, SSD_CONV_CH), small.reshape(bsz, seq, LANES),
                       conv_w[l], conv_b[l][None, :], hvec, dskip_x, consts, ssd_rows)

        f = fcum[:, :, SSD_HEADS:SSD_HEADS + ATT_HEADS].reshape(bsz, seq, ATT_HEADS // 2, 2)
        fcol = f.transpose(0, 2, 1, 3)
        frow = f.transpose(0, 2, 3, 1)
        o = _attention(q.reshape(bsz, seq, ATT_INNER), k.reshape(bsz, seq, ATT_INNER),
                       v.reshape(bsz, seq, ATT_INNER), fcol, frow, tq)

        w_r = _pad_lanes(jnp.concatenate([w_router_expert[l], w_router_group[l]], axis=1))
        b_r = _pad_lanes(jnp.concatenate([b_router_expert[l], b_router_group[l]])[None, :])
        x1, h2, gates = _mixout(
            y.reshape(t, SSD_INNER), z, o.reshape(t, ATT_INNER), gs, ga, x2, mod[l],
            ssd_norm_w[l][None, :], w_ssd_o[l].astype(BF16), w_att_o[l].astype(BF16),
            w_o[l].astype(BF16), ln1_g[l][None, :], ln1_b[l][None, :], w_r, b_r, seq, tm_proj)

        x2 = _moe(h2, gates, x1, mod[l], w_gate[l].astype(BF16), w_up[l].astype(BF16),
                  w_down[l].astype(BF16), ln2_g[l][None, :], ln2_b[l][None, :], seq, tm_moe)
    return x2.reshape(bsz, seq, d)
```

```python
import functools
import math

import numpy as np
import jax
import jax.numpy as jnp
from jax import lax
from jax.experimental import pallas as pl
from jax.experimental.pallas import tpu as pltpu

SSD_HEAD_DIM = 64
SSD_HEADS = 16
SSD_GROUPS = 4
SSD_STATE = 128
SSD_CHUNK = 128
SSD_CONV = 4
SSD_INNER = SSD_HEADS * SSD_HEAD_DIM
SSD_BC = SSD_GROUPS * SSD_STATE
SSD_CONV_CH = SSD_INNER + 2 * SSD_BC
ATT_HEAD_DIM = 64
ATT_HEADS = 8
ATT_INNER = ATT_HEADS * ATT_HEAD_DIM
N_GROUPS = 4
EXPERTS_PER_GROUP = 4
N_EXPERTS = 16
N_MOD = 6
DEPTH_FOR_DEEPNORM = 4
DEEPNORM_ALPHA = (2 * DEPTH_FOR_DEEPNORM) ** 0.25
LN_EPS = 1e-5

LANES = 128
VMEM_LIMIT = 56 * 1024 * 1024

F32 = jnp.float32
BF16 = jnp.bfloat16
HIGHEST = lax.Precision.HIGHEST


def _sigmoid(x):
    return 1.0 / (1.0 + jnp.exp(-x))


def _softplus(x):
    return jnp.maximum(x, 0.0) + jnp.log(1.0 + jnp.exp(-jnp.abs(x)))


def _layer_norm_rows(x):
    mu = jnp.mean(x, axis=-1, keepdims=True)
    xc = x - mu
    var = jnp.mean(xc * xc, axis=-1, keepdims=True)
    return xc * lax.rsqrt(var + LN_EPS)


def _const_spec(shape):
    n = len(shape)
    return pl.BlockSpec(shape, lambda *_: (0,) * n)


def _mod_kernel(c_ref, w_ref, b_ref, o_ref):
    c = c_ref[...]
    ca = c * _sigmoid(c)
    o_ref[0] = jnp.dot(ca, w_ref[0], precision=HIGHEST, preferred_element_type=F32) + b_ref[0]


def _modulation(c, w_mod, b_mod):
    depth, d, n = w_mod.shape
    bsz = c.shape[0]
    tn = 1024
    return pl.pallas_call(
        _mod_kernel,
        grid=(depth, n // tn),
        in_specs=[
            pl.BlockSpec((bsz, d), lambda l, j: (0, 0)),
            pl.BlockSpec((1, d, tn), lambda l, j: (l, 0, j)),
            pl.BlockSpec((1, 1, tn), lambda l, j: (l, 0, j)),
        ],
        out_specs=pl.BlockSpec((1, bsz, tn), lambda l, j: (l, 0, j)),
        out_shape=jax.ShapeDtypeStruct((depth, bsz, n), F32),
        compiler_params=pltpu.CompilerParams(
            dimension_semantics=("arbitrary", "arbitrary"), vmem_limit_bytes=VMEM_LIMIT),
        name="modulation",
    )(c, w_mod, b_mod.reshape(depth, 1, n))


def _inproj_kernel(x_ref, mod_ref, wz_ref, wxbc_ref, wq_ref, wk_ref, wv_ref, wgs_ref, wga_ref,
                   wsm_ref, z_ref, xbc_ref, q_ref, k_ref, v_ref, gs_ref, ga_ref, sm_ref):
    x = x_ref[...]
    shift = mod_ref[0:1, :]
    scale = mod_ref[1:2, :]
    h = _layer_norm_rows(x) * (1.0 + scale) + shift
    hb = h.astype(BF16)
    for w_ref, o_ref in ((wz_ref, z_ref), (wxbc_ref, xbc_ref), (wq_ref, q_ref), (wk_ref, k_ref),
                         (wv_ref, v_ref), (wgs_ref, gs_ref), (wga_ref, ga_ref)):
        o_ref[...] = jnp.dot(hb, w_ref[...], preferred_element_type=F32).astype(o_ref.dtype)
    sm_ref[...] = jnp.dot(h, wsm_ref[...], precision=HIGHEST, preferred_element_type=F32)


def _inproj(x2, mod_l, weights, w_small, seq, tm):
    t, d = x2.shape
    tiles_per_batch = seq // tm
    widths = [w.shape[1] for w in weights]
    in_specs = [
        pl.BlockSpec((tm, d), lambda i: (i, 0)),
        pl.BlockSpec((None, N_MOD, d), lambda i: (i // tiles_per_batch, 0, 0)),
    ]
    in_specs += [_const_spec(w.shape) for w in weights]
    in_specs += [_const_spec(w_small.shape)]
    out_specs = [pl.BlockSpec((tm, n), lambda i: (i, 0)) for n in widths]
    out_specs += [pl.BlockSpec((tm, LANES), lambda i: (i, 0))]
    out_shape = [jax.ShapeDtypeStruct((t, n), BF16) for n in widths]
    out_shape += [jax.ShapeDtypeStruct((t, LANES), F32)]
    return pl.pallas_call(
        _inproj_kernel,
        grid=(t // tm,),
        in_specs=in_specs,
        out_specs=out_specs,
        out_shape=out_shape,
        compiler_params=pltpu.CompilerParams(
            dimension_semantics=("arbitrary",), vmem_limit_bytes=VMEM_LIMIT),
        name="inproj",
    )(x2, mod_l, *weights, w_small)


def _ssd_constants():
    L = SSD_CHUNK
    sh = np.zeros((3 * L, 2 * L), np.float32)
    for j in range(1, SSD_CONV):
        for t in range(L):
            sh[(j - 1) * L + t, L + t - j] = 1.0
    tril = np.tril(np.ones((L, L), np.float32))
    e3 = np.zeros((LANES, SSD_INNER), np.float32)
    for piece in range(3):
        for h in range(SSD_HEADS):
            e3[32 * piece + h, h * SSD_HEAD_DIM:(h + 1) * SSD_HEAD_DIM] = 1.0
    r = SSD_HEADS // SSD_GROUPS
    bd = np.zeros((r * L, r * SSD_HEAD_DIM), np.float32)
    for i in range(r):
        bd[i * L:(i + 1) * L, i * SSD_HEAD_DIM:(i + 1) * SSD_HEAD_DIM] = 1.0
    return (jnp.asarray(sh, BF16), jnp.asarray(tril, F32), jnp.asarray(e3, BF16),
            jnp.asarray(bd, BF16))


def _split3_bf16(a):
    hi = a.astype(BF16)
    r1 = a - hi.astype(F32)
    mid = r1.astype(BF16)
    lo = (r1 - mid.astype(F32)).astype(BF16)
    return hi, mid, lo


def _expand_heads(xm, e3):
    hi, mid, lo = _split3_bf16(xm)
    packed = (hi.astype(F32) + pltpu.roll(mid.astype(F32), 32, 1)
              + pltpu.roll(lo.astype(F32), 64, 1)).astype(BF16)
    return jnp.dot(packed, e3, preferred_element_type=F32)


def _ssd_kernel(xbc_ref, sm_ref, cw_ref, cb_ref, hv_ref, dskip_ref, sh_ref, tril_ref, e3_ref,
                bd_ref, y_ref, fcum_ref, uext, state, fcarry, *, rows):
    L = SSD_CHUNK
    j = pl.program_id(1)

    @pl.when(j == 0)
    def _():
        uext[0:L, :] = jnp.zeros((L, SSD_CONV_CH), BF16)
        state[...] = jnp.zeros_like(state)
        fcarry[...] = jnp.zeros_like(fcarry)

    uext[L:L + rows, :] = xbc_ref[0]

    lane = lax.broadcasted_iota(jnp.int32, (1, LANES), 1)
    is_dt = lane < SSD_HEADS
    is_f = jnp.logical_and(lane >= SSD_HEADS, lane < SSD_HEADS + ATT_HEADS)
    a_neg = -jnp.exp(hv_ref[1:2, :])
    bias = hv_ref[0:1, :]
    rr = lax.broadcasted_iota(jnp.int32, (L, L), 0)
    cc = lax.broadcasted_iota(jnp.int32, (L, L), 1)
    causal = rr >= cc
    gw = (SSD_HEADS // SSD_GROUPS) * SSD_HEAD_DIM

    for c in range(rows // L):
        r0 = c * L
        ue = uext[r0:r0 + 2 * L, :]
        ucur = ue[L:2 * L, :].astype(F32)
        shifted = jnp.dot(sh_ref[...], ue, preferred_element_type=F32)
        conv = (cw_ref[3:4, :] * ucur + cw_ref[2:3, :] * shifted[0:L]
                + cw_ref[1:2, :] * shifted[L:2 * L] + cw_ref[0:1, :] * shifted[2 * L:3 * L]
                + cb_ref[...])
        act = conv * _sigmoid(conv)
        xs = act[:, :SSD_INNER]
        bm = act[:, SSD_INNER:SSD_INNER + SSD_BC]
        cm = act[:, SSD_INNER + SSD_BC:]

        pre = sm_ref[0, r0:r0 + L, :] + bias
        dt = jnp.where(is_dt, _softplus(pre), 0.0)
        log_f = jnp.minimum(pre, 0.0) - jnp.log(1.0 + jnp.exp(-jnp.abs(pre)))
        comb = jnp.where(is_dt, dt * a_neg, jnp.where(is_f, log_f, 0.0))
        cs = jnp.dot(tril_ref[...], comb, precision=HIGHEST, preferred_element_type=F32)
        fc = cs + fcarry[...]
        fcarry[...] = jnp.where(is_f, fc[L - 1:L, :], 0.0)
        f_hi, f_mid, f_lo = _split3_bf16(jnp.where(is_f, fc * math.log2(math.e), 0.0))
        fcum_ref[0, r0:r0 + L, :] = (
            f_hi.astype(F32) + pltpu.roll(f_mid.astype(F32), 32, 1)
            + pltpu.roll(f_lo.astype(F32), 64, 1)).astype(BF16)

        a_cs = jnp.where(is_dt, cs, 0.0)
        a_cs_x = _expand_heads(a_cs, e3_ref[...])
        dt_x = _expand_heads(dt, e3_ref[...])
        a_last_x = a_cs_x[L - 1:L, :]
        xdt = xs * dt_x
        xdt_b = xdt.astype(BF16)
        xend_b = (xdt * jnp.exp(a_last_x - a_cs_x)).astype(BF16)
        e_acs = jnp.exp(a_cs_x)
        e_last = jnp.exp(a_last_x)
        a_cs_t = jnp.transpose(a_cs)

        y_groups = []
        for g in range(SSD_GROUPS):
            bg = bm[:, g * SSD_STATE:(g + 1) * SSD_STATE]
            cg_b = cm[:, g * SSD_STATE:(g + 1) * SSD_STATE].astype(BF16)
            cb = lax.dot_general(cg_b, bg.astype(BF16), (((1,), (1,)), ((), ())),
                                 preferred_element_type=F32)
            ms = []
            for r in range(SSD_HEADS // SSD_GROUPS):
                h = g * (SSD_HEADS // SSD_GROUPS) + r
                seg = a_cs[:, h:h + 1] - a_cs_t[h:h + 1, :]
                dec = jnp.exp(jnp.where(causal, seg, -jnp.inf))
                ms.append((cb * dec).astype(BF16))
            m_cat = jnp.concatenate(ms, axis=1)
            xg = xdt_b[:, g * gw:(g + 1) * gw]
            x_bd = jnp.concatenate([xg] * (SSD_HEADS // SSD_GROUPS), axis=0) * bd_ref[...]
            y_diag = jnp.dot(m_cat, x_bd, preferred_element_type=F32)
            st = state[:, g * gw:(g + 1) * gw]
            y_off = jnp.dot(cg_b, st.astype(BF16), preferred_element_type=F32)
            y_groups.append(y_diag + y_off * e_acs[:, g * gw:(g + 1) * gw])
            new_st = jnp.dot(jnp.transpose(bg).astype(BF16), xend_b[:, g * gw:(g + 1) * gw],
                             preferred_element_type=F32)
            state[:, g * gw:(g + 1) * gw] = st * e_last[:, g * gw:(g + 1) * gw] + new_st
        y = jnp.concatenate(y_groups, axis=1) + dskip_ref[...] * xs
        y_ref[0, r0:r0 + L, :] = y.astype(y_ref.dtype)

    uext[0:L, :] = uext[rows:rows + L, :]


def _ssd(xbc, small, conv_w, conv_b, hvec, dskip_x, consts, rows):
    bsz, seq, _ = xbc.shape
    sh, tril, e3, bd = consts
    kern = functools.partial(_ssd_kernel, rows=rows)
    return pl.pallas_call(
        kern,
        grid=(bsz, seq // rows),
        in_specs=[
            pl.BlockSpec((1, rows, SSD_CONV_CH), lambda b, j: (b, j, 0)),
            pl.BlockSpec((1, rows, LANES), lambda b, j: (b, j, 0)),
            _const_spec(conv_w.shape), _const_spec(conv_b.shape), _const_spec(hvec.shape),
            _const_spec(dskip_x.shape), _const_spec(sh.shape), _const_spec(tril.shape),
            _const_spec(e3.shape), _const_spec(bd.shape),
        ],
        out_specs=[
            pl.BlockSpec((1, rows, SSD_INNER), lambda b, j: (b, j, 0)),
            pl.BlockSpec((1, rows, LANES), lambda b, j: (b, j, 0)),
        ],
        out_shape=[
            jax.ShapeDtypeStruct((bsz, seq, SSD_INNER), BF16),
            jax.ShapeDtypeStruct((bsz, seq, LANES), BF16),
        ],
        scratch_shapes=[
            pltpu.VMEM((SSD_CHUNK + rows, SSD_CONV_CH), BF16),
            pltpu.VMEM((SSD_STATE, SSD_INNER), F32),
            pltpu.VMEM((1, LANES), F32),
        ],
        compiler_params=pltpu.CompilerParams(
            dimension_semantics=("arbitrary", "arbitrary"), vmem_limit_bytes=VMEM_LIMIT),
        name="ssd",
    )(xbc, small, conv_w, conv_b, hvec, dskip_x, sh, tril, e3, bd)


def _attn_kernel(qt_ref, k_ref, vt_ref, o_ref, s_sc, *, tq, tk):
    qi = pl.program_id(2)
    r = tq // tk
    n_full = qi * r
    key_idx = lax.broadcasted_iota(jnp.int32, (tk, tq), 0)
    qry_idx = lax.broadcasted_iota(jnp.int32, (tk, tq), 1)

    def scores(kblk):
        start = pl.multiple_of(kblk * tk, tk)
        kb = k_ref[0, pl.ds(start, tk), :]
        return [jnp.dot(kb[:, hd * LANES:(hd + 1) * LANES], qt_ref[0, hd],
                        preferred_element_type=F32) for hd in range(2)]

    def prepare(s_list, diag):
        mcols = []
        for hd in range(2):
            s_t = s_list[hd]
            if diag is not None:
                s_t = jnp.where(key_idx + diag * tk <= qry_idx, s_t, -jnp.inf)
            s_sc[hd] = s_t
            mcols.append(jnp.max(s_t, axis=0, keepdims=True))
        return mcols

    def consume(kblk, mcols, carry):
        start = pl.multiple_of(kblk * tk, tk)
        vb = vt_ref[0, :, pl.ds(start, tk)]
        new = []
        for hd in range(2):
            m_old, l_old, acc = carry[hd]
            m_new = jnp.maximum(m_old, mcols[hd])
            p_t = jnp.exp2(s_sc[hd] - m_new)
            alpha = jnp.exp2(m_old - m_new)
            l_new = alpha * l_old + jnp.sum(p_t, axis=0, keepdims=True)
            pv = jnp.dot(vb[hd * ATT_HEAD_DIM:(hd + 1) * ATT_HEAD_DIM, :], p_t.astype(BF16),
                         preferred_element_type=F32)
            new.append((m_new, l_new, alpha * acc + pv))
        return tuple(new)

    carry = tuple((jnp.full((1, tq), -jnp.inf, F32), jnp.zeros((1, tq), F32),
                   jnp.zeros((ATT_HEAD_DIM, tq), F32)) for _ in range(2))
    mcols = prepare(scores(n_full), 0)
    for d in range(r):
        s_next = scores(n_full + d + 1 if d + 1 < r else 0)
        carry = consume(n_full + d, mcols, carry)
        mcols = prepare(s_next, d + 1 if d + 1 < r else None)

    def body(jb, state):
        carry, mcols = state
        s_next = scores(jnp.minimum(jb + 1, n_full - 1))
        carry = consume(jb, mcols, carry)
        return carry, tuple(prepare(s_next, None))

    carry, _ = lax.fori_loop(0, n_full, body, (carry, tuple(mcols)))
    o_t = jnp.concatenate([carry[0][2] / carry[0][1], carry[1][2] / carry[1][1]], axis=0)
    o_ref[0] = jnp.transpose(o_t).astype(o_ref.dtype)


def _attention(q_aug_t, k_aug, v_t, tq, tk):
    bsz, _, _, seq = q_aug_t.shape
    pairs = ATT_HEADS // 2
    kern = functools.partial(_attn_kernel, tq=tq, tk=tk)
    return pl.pallas_call(
        kern,
        grid=(bsz, pairs, seq // tq),
        in_specs=[
            pl.BlockSpec((1, 2, LANES, tq), lambda b, p, i: (b, p, 0, i)),
            pl.BlockSpec((1, seq, 2 * LANES), lambda b, p, i: (b, 0, p)),
            pl.BlockSpec((1, 2 * ATT_HEAD_DIM, seq), lambda b, p, i: (b, p, 0)),
        ],
        out_specs=pl.BlockSpec((1, tq, LANES), lambda b, p, i: (b, i, p)),
        out_shape=jax.ShapeDtypeStruct((bsz, seq, ATT_INNER), BF16),
        scratch_shapes=[pltpu.VMEM((2, tk, tq), F32)],
        compiler_params=pltpu.CompilerParams(
            dimension_semantics=("arbitrary", "arbitrary", "arbitrary"),
            vmem_limit_bytes=VMEM_LIMIT),
        name="fox_attention",
    )(q_aug_t, k_aug, v_t)


def _attention_operands(q, k, v, f_pieces, bsz, seq):
    hi = f_pieces[:, :, SSD_HEADS:SSD_HEADS + ATT_HEADS]
    mid = f_pieces[:, :, 32 + SSD_HEADS:32 + SSD_HEADS + ATT_HEADS]
    lo = f_pieces[:, :, 64 + SSD_HEADS:64 + SSD_HEADS + ATT_HEADS]
    one = jnp.ones_like(hi)
    pad = LANES - ATT_HEAD_DIM - 6
    q4 = q.reshape(bsz, seq, ATT_HEADS, ATT_HEAD_DIM)
    k4 = k.reshape(bsz, seq, ATT_HEADS, ATT_HEAD_DIM)
    zeros = jnp.zeros((bsz, seq, ATT_HEADS, pad), BF16)
    q_aug = jnp.concatenate(
        [q4, jnp.stack([hi, mid, lo, one, one, one], axis=-1), zeros], axis=-1)
    k_aug = jnp.concatenate(
        [k4, jnp.stack([one, one, one, -hi, -mid, -lo], axis=-1), zeros], axis=-1)
    q_aug_t = q_aug.transpose(0, 2, 3, 1)
    k_aug = k_aug.reshape(bsz, seq, ATT_HEADS * LANES)
    v_t = v.reshape(bsz, seq, ATT_INNER).transpose(0, 2, 1)
    return q_aug_t, k_aug, v_t


def _route(logits):
    lane = lax.broadcasted_iota(jnp.int32, logits.shape, 1)
    neg = -jnp.inf
    big = jnp.int32(1 << 20)
    is_g = jnp.logical_and(lane >= N_EXPERTS, lane < N_EXPERTS + N_GROUPS)
    gl = jnp.where(is_g, logits, neg)
    gmax = jnp.max(gl, axis=1, keepdims=True)
    gsum = jnp.sum(jnp.exp(gl - gmax), axis=1, keepdims=True)
    g_p = 1.0 / gsum
    g_idx = jnp.min(jnp.where(gl == gmax, lane, big), axis=1, keepdims=True) - N_EXPERTS
    lo = g_idx * EXPERTS_PER_GROUP
    in_group = jnp.logical_and(lane >= lo, lane < lo + EXPERTS_PER_GROUP)
    el = jnp.where(in_group, logits, neg)
    m1 = jnp.max(el, axis=1, keepdims=True)
    i1 = jnp.min(jnp.where(el == m1, lane, big), axis=1, keepdims=True)
    el2 = jnp.where(lane == i1, neg, el)
    m2 = jnp.max(el2, axis=1, keepdims=True)
    i2 = jnp.min(jnp.where(el2 == m2, lane, big), axis=1, keepdims=True)
    e2 = jnp.exp(m2 - m1)
    w1 = g_p / (1.0 + e2)
    w2 = g_p * e2 / (1.0 + e2)
    return jnp.where(lane == i1, w1, 0.0) + jnp.where(lane == i2, w2, 0.0)


def _mixout_kernel(y_ref, z_ref, o_ref, gs_ref, ga_ref, x_ref, mod_ref, nw_ref, wso_ref, wao_ref,
                   wo_ref, lng_ref, lnb_ref, wr_ref, br_ref, x1_ref, h2_ref, gates_ref):
    g = y_ref[...].astype(F32)
    z = z_ref[...].astype(F32)
    g = g * (z * _sigmoid(z))
    g = g * lax.rsqrt(jnp.mean(g * g, axis=-1, keepdims=True) + LN_EPS) * nw_ref[...]
    y_ssd = jnp.dot(g.astype(BF16), wso_ref[...], preferred_element_type=F32)
    y_att = jnp.dot(o_ref[...], wao_ref[...], preferred_element_type=F32)
    merged = (_sigmoid(gs_ref[...].astype(F32)) * y_ssd
              + _sigmoid(ga_ref[...].astype(F32)) * y_att)
    mix = jnp.dot(merged.astype(BF16), wo_ref[...], preferred_element_type=F32)
    gate1 = mod_ref[2:3, :]
    x1 = _layer_norm_rows(DEEPNORM_ALPHA * x_ref[...] + (1.0 + gate1) * mix)
    x1 = x1 * lng_ref[...] + lnb_ref[...]
    x1_ref[...] = x1
    h2 = _layer_norm_rows(x1) * (1.0 + mod_ref[4:5, :]) + mod_ref[3:4, :]
    h2_ref[...] = h2.astype(BF16)
    logits = jnp.dot(h2, wr_ref[...], precision=HIGHEST, preferred_element_type=F32) + br_ref[...]
    gates_ref[...] = _route(logits)


def _mixout(y, z, o, gs, ga, x2, mod_l, norm_w, w_ssd_o, w_att_o, w_o, ln_g, ln_b, w_r, b_r,
            seq, tm):
    t, d = x2.shape
    tiles_per_batch = seq // tm
    row = lambda n: pl.BlockSpec((tm, n), lambda i: (i, 0))
    return pl.pallas_call(
        _mixout_kernel,
        grid=(t // tm,),
        in_specs=[
            row(SSD_INNER), row(SSD_INNER), row(ATT_INNER), row(d), row(d), row(d),
            pl.BlockSpec((None, N_MOD, d), lambda i: (i // tiles_per_batch, 0, 0)),
            _const_spec(norm_w.shape), _const_spec(w_ssd_o.shape), _const_spec(w_att_o.shape),
            _const_spec(w_o.shape), _const_spec(ln_g.shape), _const_spec(ln_b.shape),
            _const_spec(w_r.shape), _const_spec(b_r.shape),
        ],
        out_specs=[row(d), row(d), row(LANES)],
        out_shape=[
            jax.ShapeDtypeStruct((t, d), F32),
            jax.ShapeDtypeStruct((t, d), BF16),
            jax.ShapeDtypeStruct((t, LANES), F32),
        ],
        compiler_params=pltpu.CompilerParams(
            dimension_semantics=("arbitrary",), vmem_limit_bytes=VMEM_LIMIT),
        name="mixer_out",
    )(y, z, o, gs, ga, x2, mod_l, norm_w, w_ssd_o, w_att_o, w_o, ln_g, ln_b, w_r, b_r)


def _moe_kernel(h_ref, gates_ref, x1_ref, mod_ref, wg_ref, wu_ref, wd_ref, lng_ref, lnb_ref,
                o_ref, acc_ref):
    e = pl.program_id(1)

    @pl.when(e == 0)
    def _():
        acc_ref[...] = jnp.zeros_like(acc_ref)

    h = h_ref[...]
    a = jnp.dot(h, wg_ref[0], preferred_element_type=F32)
    u = jnp.dot(h, wu_ref[0], preferred_element_type=F32)
    hid = (a * _sigmoid(a)) * u
    lane = lax.broadcasted_iota(jnp.int32, (1, LANES), 1)
    gate = jnp.sum(jnp.where(lane == e, gates_ref[...], 0.0), axis=1, keepdims=True)
    acc_ref[...] += gate * jnp.dot(hid.astype(BF16), wd_ref[0], preferred_element_type=F32)

    @pl.when(e == pl.num_programs(1) - 1)
    def _():
        gate2 = mod_ref[5:6, :]
        xn = _layer_norm_rows(DEEPNORM_ALPHA * x1_ref[...] + (1.0 + gate2) * acc_ref[...])
        o_ref[...] = xn * lng_ref[...] + lnb_ref[...]


def _moe(h2, gates, x1, mod_l, w_gate, w_up, w_down, ln_g, ln_b, seq, tm):
    t, d = x1.shape
    n_e, _, ff = w_gate.shape
    tiles_per_batch = seq // tm
    return pl.pallas_call(
        _moe_kernel,
        grid=(t // tm, n_e),
        in_specs=[
            pl.BlockSpec((tm, d), lambda i, e: (i, 0)),
            pl.BlockSpec((tm, LANES), lambda i, e: (i, 0)),
            pl.BlockSpec((tm, d), lambda i, e: (i, 0)),
            pl.BlockSpec((None, N_MOD, d), lambda i, e: (i // tiles_per_batch, 0, 0)),
            pl.BlockSpec((1, d, ff), lambda i, e: (e, 0, 0)),
            pl.BlockSpec((1, d, ff), lambda i, e: (e, 0, 0)),
            pl.BlockSpec((1, ff, d), lambda i, e: (e, 0, 0)),
            pl.BlockSpec((1, d), lambda i, e: (0, 0)),
            pl.BlockSpec((1, d), lambda i, e: (0, 0)),
        ],
        out_specs=pl.BlockSpec((tm, d), lambda i, e: (i, 0)),
        out_shape=jax.ShapeDtypeStruct((t, d), F32),
        scratch_shapes=[pltpu.VMEM((tm, d), F32)],
        compiler_params=pltpu.CompilerParams(
            dimension_semantics=("arbitrary", "arbitrary"), vmem_limit_bytes=VMEM_LIMIT),
        name="moe",
    )(h2, gates, x1, mod_l, w_gate, w_up, w_down, ln_g, ln_b)


def _pad_lanes(a):
    return jnp.pad(a, ((0, 0), (0, LANES - a.shape[1])))


def kernel(x, c, w_mod, b_mod, w_in, conv_w, conv_b, dt_bias, a_log, d_skip, ssd_norm_w, forget_b,
           w_ssd_o, w_att_o, w_o, ln1_g, ln1_b, w_router_group, b_router_group, w_router_expert,
           b_router_expert, w_gate, w_up, w_down, ln2_g, ln2_b):
    bsz, seq, d = x.shape
    depth = w_mod.shape[0]
    t = bsz * seq
    tm_proj = min(512, seq)
    tm_moe = min(1024, seq)
    ssd_rows = min(512, seq)
    tq = min(512, seq)
    tk = min(512, seq)

    mod = _modulation(c, w_mod, b_mod).reshape(depth, bsz, N_MOD, d)
    consts = _ssd_constants()
    in_sizes = (SSD_INNER, SSD_CONV_CH, SSD_HEADS, ATT_INNER, ATT_INNER, ATT_INNER, ATT_HEADS, d, d)
    offs = np.concatenate([[0], np.cumsum(in_sizes)]).tolist()

    x2 = x.reshape(t, d)
    for l in range(depth):
        cols = [w_in[l][:, offs[i]:offs[i + 1]] for i in range(len(in_sizes))]
        wz, wxbc, wdt, wq, wk, wv, wf, wgs, wga = cols
        wq = wq * (math.log2(math.e) * ATT_HEAD_DIM ** -0.5)
        big = [w.astype(BF16) for w in (wz, wxbc, wq, wk, wv, wgs, wga)]
        w_small = _pad_lanes(jnp.concatenate([wdt, wf], axis=1))
        z, xbc, q, k, v, gs, ga, small = _inproj(x2, mod[l], big, w_small, seq, tm_proj)

        hvec = jnp.concatenate([
            _pad_lanes(jnp.concatenate([dt_bias[l], forget_b[l]])[None, :]),
            _pad_lanes(a_log[l][None, :]),
            jnp.zeros((6, LANES), F32)], axis=0)
        dskip_x = jnp.repeat(d_skip[l], SSD_HEAD_DIM)[None, :]
        y, fcum = _ssd(xbc.reshape(bsz, seq, SSD_CONV_CH), small.reshape(bsz, seq, LANES),
                       conv_w[l], conv_b[l][None, :], hvec, dskip_x, consts, ssd_rows)

        o = _attention(*_attention_operands(q, k, v, fcum, bsz, seq), tq, tk)

        w_r = _pad_lanes(jnp.concatenate([w_router_expert[l], w_router_group[l]], axis=1))
        b_r = _pad_lanes(jnp.concatenate([b_router_expert[l], b_router_group[l]])[None, :])
        x1, h2, gates = _mixout(
            y.reshape(t, SSD_INNER), z, o.reshape(t, ATT_INNER), gs, ga, x2, mod[l],
            ssd_norm_w[l][None, :], w_ssd_o[l].astype(BF16), w_att_o[l].astype(BF16),
            w_o[l].astype(BF16), ln1_g[l][None, :], ln1_b[l][None, :], w_r, b_r, seq, tm_proj)

        x2 = _moe(h2, gates, x1, mod[l], w_gate[l].astype(BF16), w_up[l].astype(BF16),
                  w_down[l].astype(BF16), ln2_g[l][None, :], ln2_b[l][None, :], seq, tm_moe)
    return x2.reshape(bsz, seq, d)
```

```python
import functools
import math

import numpy as np
import jax
import jax.numpy as jnp
from jax import lax
from jax.experimental import pallas as pl
from jax.experimental.pallas import tpu as pltpu

SSD_HEAD_DIM = 64
SSD_HEADS = 16
SSD_GROUPS = 4
SSD_STATE = 128
SSD_CHUNK = 128
SSD_CONV = 4
SSD_INNER = SSD_HEADS * SSD_HEAD_DIM
SSD_BC = SSD_GROUPS * SSD_STATE
SSD_CONV_CH = SSD_INNER + 2 * SSD_BC
ATT_HEAD_DIM = 64
ATT_HEADS = 8
ATT_INNER = ATT_HEADS * ATT_HEAD_DIM
N_GROUPS = 4
EXPERTS_PER_GROUP = 4
N_EXPERTS = 16
N_MOD = 6
DEPTH_FOR_DEEPNORM = 4
DEEPNORM_ALPHA = (2 * DEPTH_FOR_DEEPNORM) ** 0.25
LN_EPS = 1e-5

LANES = 128
VMEM_LIMIT = 56 * 1024 * 1024

F32 = jnp.float32
BF16 = jnp.bfloat16
HIGHEST = lax.Precision.HIGHEST


def _sigmoid(x):
    return 1.0 / (1.0 + jnp.exp(-x))


def _softplus(x):
    return jnp.maximum(x, 0.0) + jnp.log(1.0 + jnp.exp(-jnp.abs(x)))


def _layer_norm_rows(x):
    mu = jnp.mean(x, axis=-1, keepdims=True)
    xc = x - mu
    var = jnp.mean(xc * xc, axis=-1, keepdims=True)
    return xc * lax.rsqrt(var + LN_EPS)


def _const_spec(shape):
    n = len(shape)
    return pl.BlockSpec(shape, lambda *_: (0,) * n)


def _mod_kernel(c_ref, w_ref, b_ref, o_ref):
    c = c_ref[...]
    ca = c * _sigmoid(c)
    o_ref[0] = jnp.dot(ca, w_ref[0], precision=HIGHEST, preferred_element_type=F32) + b_ref[0]


def _modulation(c, w_mod, b_mod):
    depth, d, n = w_mod.shape
    bsz = c.shape[0]
    tn = 1024
    return pl.pallas_call(
        _mod_kernel,
        grid=(depth, n // tn),
        in_specs=[
            pl.BlockSpec((bsz, d), lambda l, j: (0, 0)),
            pl.BlockSpec((1, d, tn), lambda l, j: (l, 0, j)),
            pl.BlockSpec((1, 1, tn), lambda l, j: (l, 0, j)),
        ],
        out_specs=pl.BlockSpec((1, bsz, tn), lambda l, j: (l, 0, j)),
        out_shape=jax.ShapeDtypeStruct((depth, bsz, n), F32),
        compiler_params=pltpu.CompilerParams(
            dimension_semantics=("arbitrary", "arbitrary"), vmem_limit_bytes=VMEM_LIMIT),
        name="modulation",
    )(c, w_mod, b_mod.reshape(depth, 1, n))


def _inproj_kernel(x_ref, mod_ref, wz_ref, wxbc_ref, wq_ref, wk_ref, wv_ref, wgs_ref, wga_ref,
                   wsm_ref, z_ref, xbc_ref, q_ref, k_ref, v_ref, gs_ref, ga_ref, sm_ref):
    x = x_ref[...]
    shift = mod_ref[0:1, :]
    scale = mod_ref[1:2, :]
    h = _layer_norm_rows(x) * (1.0 + scale) + shift
    hb = h.astype(BF16)
    for w_ref, o_ref in ((wz_ref, z_ref), (wxbc_ref, xbc_ref), (wq_ref, q_ref), (wk_ref, k_ref),
                         (wv_ref, v_ref), (wgs_ref, gs_ref), (wga_ref, ga_ref)):
        o_ref[...] = jnp.dot(hb, w_ref[...], preferred_element_type=F32).astype(o_ref.dtype)
    sm_ref[...] = jnp.dot(h, wsm_ref[...], precision=HIGHEST, preferred_element_type=F32)


def _inproj(x2, mod_l, weights, w_small, seq, tm):
    t, d = x2.shape
    tiles_per_batch = seq // tm
    widths = [w.shape[1] for w in weights]
    in_specs = [
        pl.BlockSpec((tm, d), lambda i: (i, 0)),
        pl.BlockSpec((None, N_MOD, d), lambda i: (i // tiles_per_batch, 0, 0)),
    ]
    in_specs += [_const_spec(w.shape) for w in weights]
    in_specs += [_const_spec(w_small.shape)]
    out_specs = [pl.BlockSpec((tm, n), lambda i: (i, 0)) for n in widths]
    out_specs += [pl.BlockSpec((tm, LANES), lambda i: (i, 0))]
    out_shape = [jax.ShapeDtypeStruct((t, n), BF16) for n in widths]
    out_shape += [jax.ShapeDtypeStruct((t, LANES), F32)]
    return pl.pallas_call(
        _inproj_kernel,
        grid=(t // tm,),
        in_specs=in_specs,
        out_specs=out_specs,
        out_shape=out_shape,
        compiler_params=pltpu.CompilerParams(
            dimension_semantics=("arbitrary",), vmem_limit_bytes=VMEM_LIMIT),
        name="inproj",
    )(x2, mod_l, *weights, w_small)


def _ssd_constants():
    L = SSD_CHUNK
    sh = np.zeros((3 * L, 2 * L), np.float32)
    for j in range(1, SSD_CONV):
        for t in range(L):
            sh[(j - 1) * L + t, L + t - j] = 1.0
    tril = np.tril(np.ones((L, L), np.float32))
    e3 = np.zeros((LANES, SSD_INNER), np.float32)
    for piece in range(3):
        for h in range(SSD_HEADS):
            e3[32 * piece + h, h * SSD_HEAD_DIM:(h + 1) * SSD_HEAD_DIM] = 1.0
    r = SSD_HEADS // SSD_GROUPS
    bd = np.zeros((r * L, r * SSD_HEAD_DIM), np.float32)
    for i in range(r):
        bd[i * L:(i + 1) * L, i * SSD_HEAD_DIM:(i + 1) * SSD_HEAD_DIM] = 1.0
    return (jnp.asarray(sh, BF16), jnp.asarray(tril, F32), jnp.asarray(e3, BF16),
            jnp.asarray(bd, BF16))


def _split3_bf16(a):
    hi = a.astype(BF16)
    r1 = a - hi.astype(F32)
    mid = r1.astype(BF16)
    lo = (r1 - mid.astype(F32)).astype(BF16)
    return hi, mid, lo


def _expand_heads(xm, e3):
    hi, mid, lo = _split3_bf16(xm)
    packed = (hi.astype(F32) + pltpu.roll(mid.astype(F32), 32, 1)
              + pltpu.roll(lo.astype(F32), 64, 1)).astype(BF16)
    return jnp.dot(packed, e3, preferred_element_type=F32)


def _ssd_kernel(xbc_ref, sm_ref, cw_ref, cb_ref, hv_ref, dskip_ref, sh_ref, tril_ref, e3_ref,
                bd_ref, y_ref, fcum_ref, uext, state, fcarry, *, rows):
    L = SSD_CHUNK
    j = pl.program_id(1)

    @pl.when(j == 0)
    def _():
        uext[0:L, :] = jnp.zeros((L, SSD_CONV_CH), BF16)
        state[...] = jnp.zeros_like(state)
        fcarry[...] = jnp.zeros_like(fcarry)

    uext[L:L + rows, :] = xbc_ref[0]

    lane = lax.broadcasted_iota(jnp.int32, (1, LANES), 1)
    is_dt = lane < SSD_HEADS
    is_f = jnp.logical_and(lane >= SSD_HEADS, lane < SSD_HEADS + ATT_HEADS)
    a_neg = -jnp.exp(hv_ref[1:2, :])
    bias = hv_ref[0:1, :]
    rr = lax.broadcasted_iota(jnp.int32, (L, L), 0)
    cc = lax.broadcasted_iota(jnp.int32, (L, L), 1)
    causal = rr >= cc
    gw = (SSD_HEADS // SSD_GROUPS) * SSD_HEAD_DIM

    for c in range(rows // L):
        r0 = c * L
        ue = uext[r0:r0 + 2 * L, :]
        ucur = ue[L:2 * L, :].astype(F32)
        shifted = jnp.dot(sh_ref[...], ue, preferred_element_type=F32)
        conv = (cw_ref[3:4, :] * ucur + cw_ref[2:3, :] * shifted[0:L]
                + cw_ref[1:2, :] * shifted[L:2 * L] + cw_ref[0:1, :] * shifted[2 * L:3 * L]
                + cb_ref[...])
        act = conv * _sigmoid(conv)
        xs = act[:, :SSD_INNER]
        bm = act[:, SSD_INNER:SSD_INNER + SSD_BC]
        cm = act[:, SSD_INNER + SSD_BC:]

        pre = sm_ref[0, r0:r0 + L, :] + bias
        dt = jnp.where(is_dt, _softplus(pre), 0.0)
        log_f = jnp.minimum(pre, 0.0) - jnp.log(1.0 + jnp.exp(-jnp.abs(pre)))
        comb = jnp.where(is_dt, dt * a_neg, jnp.where(is_f, log_f, 0.0))
        cs = jnp.dot(tril_ref[...], comb, precision=HIGHEST, preferred_element_type=F32)
        fc = cs + fcarry[...]
        fcarry[...] = jnp.where(is_f, fc[L - 1:L, :], 0.0)
        f_hi, f_mid, f_lo = _split3_bf16(jnp.where(is_f, fc * math.log2(math.e), 0.0))
        fcum_ref[0, r0:r0 + L, :] = (
            f_hi.astype(F32) + pltpu.roll(f_mid.astype(F32), 32, 1)
            + pltpu.roll(f_lo.astype(F32), 64, 1)
            + jnp.where(lane == LANES - 1, 1.0, 0.0)).astype(BF16)

        a_cs = jnp.where(is_dt, cs, 0.0)
        a_cs_x = _expand_heads(a_cs, e3_ref[...])
        dt_x = _expand_heads(dt, e3_ref[...])
        a_last_x = a_cs_x[L - 1:L, :]
        xdt = xs * dt_x
        xdt_b = xdt.astype(BF16)
        xend_b = (xdt * jnp.exp(a_last_x - a_cs_x)).astype(BF16)
        e_acs = jnp.exp(a_cs_x)
        e_last = jnp.exp(a_last_x)
        a_cs_t = jnp.transpose(a_cs)

        y_groups = []
        for g in range(SSD_GROUPS):
            bg = bm[:, g * SSD_STATE:(g + 1) * SSD_STATE]
            cg_b = cm[:, g * SSD_STATE:(g + 1) * SSD_STATE].astype(BF16)
            cb = lax.dot_general(cg_b, bg.astype(BF16), (((1,), (1,)), ((), ())),
                                 preferred_element_type=F32)
            ms = []
            for r in range(SSD_HEADS // SSD_GROUPS):
                h = g * (SSD_HEADS // SSD_GROUPS) + r
                seg = a_cs[:, h:h + 1] - a_cs_t[h:h + 1, :]
                dec = jnp.exp(jnp.where(causal, seg, -jnp.inf))
                ms.append((cb * dec).astype(BF16))
            m_cat = jnp.concatenate(ms, axis=1)
            xg = xdt_b[:, g * gw:(g + 1) * gw]
            x_bd = jnp.concatenate([xg] * (SSD_HEADS // SSD_GROUPS), axis=0) * bd_ref[...]
            y_diag = jnp.dot(m_cat, x_bd, preferred_element_type=F32)
            st = state[:, g * gw:(g + 1) * gw]
            y_off = jnp.dot(cg_b, st.astype(BF16), preferred_element_type=F32)
            y_groups.append(y_diag + y_off * e_acs[:, g * gw:(g + 1) * gw])
            new_st = jnp.dot(jnp.transpose(bg).astype(BF16), xend_b[:, g * gw:(g + 1) * gw],
                             preferred_element_type=F32)
            state[:, g * gw:(g + 1) * gw] = st * e_last[:, g * gw:(g + 1) * gw] + new_st
        y = jnp.concatenate(y_groups, axis=1) + dskip_ref[...] * xs
        y_ref[0, r0:r0 + L, :] = y.astype(y_ref.dtype)

    uext[0:L, :] = uext[rows:rows + L, :]


def _ssd(xbc, small, conv_w, conv_b, hvec, dskip_x, consts, rows):
    bsz, seq, _ = xbc.shape
    sh, tril, e3, bd = consts
    kern = functools.partial(_ssd_kernel, rows=rows)
    return pl.pallas_call(
        kern,
        grid=(bsz, seq // rows),
        in_specs=[
            pl.BlockSpec((1, rows, SSD_CONV_CH), lambda b, j: (b, j, 0)),
            pl.BlockSpec((1, rows, LANES), lambda b, j: (b, j, 0)),
            _const_spec(conv_w.shape), _const_spec(conv_b.shape), _const_spec(hvec.shape),
            _const_spec(dskip_x.shape), _const_spec(sh.shape), _const_spec(tril.shape),
            _const_spec(e3.shape), _const_spec(bd.shape),
        ],
        out_specs=[
            pl.BlockSpec((1, rows, SSD_INNER), lambda b, j: (b, j, 0)),
            pl.BlockSpec((1, rows, LANES), lambda b, j: (b, j, 0)),
        ],
        out_shape=[
            jax.ShapeDtypeStruct((bsz, seq, SSD_INNER), BF16),
            jax.ShapeDtypeStruct((bsz, seq, LANES), BF16),
        ],
        scratch_shapes=[
            pltpu.VMEM((SSD_CHUNK + rows, SSD_CONV_CH), BF16),
            pltpu.VMEM((SSD_STATE, SSD_INNER), F32),
            pltpu.VMEM((1, LANES), F32),
        ],
        compiler_params=pltpu.CompilerParams(
            dimension_semantics=("arbitrary", "arbitrary"), vmem_limit_bytes=VMEM_LIMIT),
        name="ssd",
    )(xbc, small, conv_w, conv_b, hvec, dskip_x, sh, tril, e3, bd)


def _attn_placement():
    pq = np.zeros((ATT_HEADS // 2, LANES, 2 * LANES), np.float32)
    pk = np.zeros_like(pq)
    for p in range(ATT_HEADS // 2):
        for hd in range(2):
            h = 2 * p + hd
            col = hd * LANES + (ATT_HEAD_DIM if hd == 0 else 0)
            for piece in range(3):
                pq[p, 32 * piece + SSD_HEADS + h, col + piece] = 1.0
                pq[p, LANES - 1, col + 3 + piece] = 1.0
                pk[p, LANES - 1, col + piece] = 1.0
                pk[p, 32 * piece + SSD_HEADS + h, col + 3 + piece] = -1.0
    return jnp.asarray(pq, BF16), jnp.asarray(pk, BF16)


def _attn_kernel(q_ref, k_ref, v_ref, fq_ref, fk_ref, pq_ref, pk_ref, o_ref, kaug_sc, s_sc,
                 *, tq, tk):
    qi = pl.program_id(2)
    seq = k_ref.shape[1]
    r = tq // tk
    n_full = qi * r
    key_idx = lax.broadcasted_iota(jnp.int32, (tk, tq), 0)
    qry_idx = lax.broadcasted_iota(jnp.int32, (tk, tq), 1)
    lane2 = lax.broadcasted_iota(jnp.int32, (1, 2 * LANES), 1)
    keep = jnp.logical_or(lane2 < ATT_HEAD_DIM, lane2 >= LANES + ATT_HEAD_DIM)

    def augment(x, f, place):
        aug = jnp.dot(f, place, preferred_element_type=F32).astype(BF16)
        return jnp.where(keep, jnp.concatenate([x, x], axis=1), aug)

    @pl.when(qi == 0)
    def _():
        def fill(c, carry):
            rows = pl.ds(pl.multiple_of(c * tk, tk), tk)
            kaug_sc[rows, :] = augment(k_ref[0, rows, :], fk_ref[0, rows, :], pk_ref[0])
            return carry
        lax.fori_loop(0, seq // tk, fill, 0)

    q_aug = augment(q_ref[0], fq_ref[0], pq_ref[0])

    def scores(kblk):
        start = pl.multiple_of(kblk * tk, tk)
        kb = kaug_sc[pl.ds(start, tk), :]
        return [lax.dot_general(kb[:, hd * LANES:(hd + 1) * LANES],
                                q_aug[:, hd * LANES:(hd + 1) * LANES],
                                (((1,), (1,)), ((), ())), preferred_element_type=F32)
                for hd in range(2)]

    def prepare(s_list, diag):
        mcols = []
        for hd in range(2):
            s_t = s_list[hd]
            if diag is not None:
                s_t = jnp.where(key_idx + diag * tk <= qry_idx, s_t, -jnp.inf)
            s_sc[hd] = s_t
            mcols.append(jnp.max(s_t, axis=0, keepdims=True))
        return mcols

    def consume(kblk, mcols, carry):
        start = pl.multiple_of(kblk * tk, tk)
        vb = v_ref[0, pl.ds(start, tk), :]
        new = []
        for hd in range(2):
            m_old, l_old, acc = carry[hd]
            m_new = jnp.maximum(m_old, mcols[hd])
            p_t = jnp.exp2(s_sc[hd] - m_new)
            alpha = jnp.exp2(m_old - m_new)
            l_new = alpha * l_old + jnp.sum(p_t, axis=0, keepdims=True)
            pv = lax.dot_general(vb, p_t.astype(BF16), (((0,), (0,)), ((), ())),
                                 preferred_element_type=F32)
            pv = pv[hd * ATT_HEAD_DIM:(hd + 1) * ATT_HEAD_DIM, :]
            new.append((m_new, l_new, alpha * acc + pv))
        return tuple(new)

    carry = tuple((jnp.full((1, tq), -jnp.inf, F32), jnp.zeros((1, tq), F32),
                   jnp.zeros((ATT_HEAD_DIM, tq), F32)) for _ in range(2))
    mcols = prepare(scores(n_full), 0)
    for d in range(r):
        s_next = scores(n_full + d + 1 if d + 1 < r else 0)
        carry = consume(n_full + d, mcols, carry)
        mcols = prepare(s_next, d + 1 if d + 1 < r else None)

    def body(jb, state):
        carry, mcols = state
        s_next = scores(jnp.minimum(jb + 1, n_full - 1))
        carry = consume(jb, mcols, carry)
        return carry, tuple(prepare(s_next, None))

    carry, _ = lax.fori_loop(0, n_full, body, (carry, tuple(mcols)))
    o_t = jnp.concatenate([carry[0][2] / carry[0][1], carry[1][2] / carry[1][1]], axis=0)
    o_ref[0] = jnp.transpose(o_t).astype(o_ref.dtype)


def _attention(q, k, v, f_pieces, placement, tq, tk):
    bsz, seq, _ = q.shape
    pairs = ATT_HEADS // 2
    pq, pk = placement
    kern = functools.partial(_attn_kernel, tq=tq, tk=tk)
    return pl.pallas_call(
        kern,
        grid=(bsz, pairs, seq // tq),
        in_specs=[
            pl.BlockSpec((1, tq, LANES), lambda b, p, i: (b, i, p)),
            pl.BlockSpec((1, seq, LANES), lambda b, p, i: (b, 0, p)),
            pl.BlockSpec((1, seq, LANES), lambda b, p, i: (b, 0, p)),
            pl.BlockSpec((1, tq, LANES), lambda b, p, i: (b, i, 0)),
            pl.BlockSpec((1, seq, LANES), lambda b, p, i: (b, 0, 0)),
            pl.BlockSpec((1, LANES, 2 * LANES), lambda b, p, i: (p, 0, 0)),
            pl.BlockSpec((1, LANES, 2 * LANES), lambda b, p, i: (p, 0, 0)),
        ],
        out_specs=pl.BlockSpec((1, tq, LANES), lambda b, p, i: (b, i, p)),
        out_shape=jax.ShapeDtypeStruct((bsz, seq, ATT_INNER), BF16),
        scratch_shapes=[pltpu.VMEM((seq, 2 * LANES), BF16), pltpu.VMEM((2, tk, tq), F32)],
        compiler_params=pltpu.CompilerParams(
            dimension_semantics=("arbitrary", "arbitrary", "arbitrary"),
            vmem_limit_bytes=VMEM_LIMIT),
        name="fox_attention",
    )(q, k, v, f_pieces, f_pieces, pq, pk)


def _route(logits):
    lane = lax.broadcasted_iota(jnp.int32, logits.shape, 1)
    neg = -jnp.inf
    big = jnp.int32(1 << 20)
    is_g = jnp.logical_and(lane >= N_EXPERTS, lane < N_EXPERTS + N_GROUPS)
    gl = jnp.where(is_g, logits, neg)
    gmax = jnp.max(gl, axis=1, keepdims=True)
    gsum = jnp.sum(jnp.exp(gl - gmax), axis=1, keepdims=True)
    g_p = 1.0 / gsum
    g_idx = jnp.min(jnp.where(gl == gmax, lane, big), axis=1, keepdims=True) - N_EXPERTS
    lo = g_idx * EXPERTS_PER_GROUP
    in_group = jnp.logical_and(lane >= lo, lane < lo + EXPERTS_PER_GROUP)
    el = jnp.where(in_group, logits, neg)
    m1 = jnp.max(el, axis=1, keepdims=True)
    i1 = jnp.min(jnp.where(el == m1, lane, big), axis=1, keepdims=True)
    el2 = jnp.where(lane == i1, neg, el)
    m2 = jnp.max(el2, axis=1, keepdims=True)
    i2 = jnp.min(jnp.where(el2 == m2, lane, big), axis=1, keepdims=True)
    e2 = jnp.exp(m2 - m1)
    w1 = g_p / (1.0 + e2)
    w2 = g_p * e2 / (1.0 + e2)
    info = jnp.where(lane == 0, i1.astype(F32), jnp.where(lane == 1, i2.astype(F32),
                     jnp.where(lane == 2, w1, jnp.where(lane == 3, w2, 0.0))))
    chosen = jnp.where(lane == i1, 1.0, jnp.where(lane == i2, 1.0, 0.0))
    return info, jnp.sum(chosen, axis=0, keepdims=True)


def _mixout_kernel(y_ref, z_ref, o_ref, gs_ref, ga_ref, x_ref, mod_ref, nw_ref, wso_ref, wao_ref,
                   wo_ref, lng_ref, lnb_ref, wr_ref, br_ref, x1_ref, h2_ref, rinfo_ref, cnt_ref):
    g = y_ref[...].astype(F32)
    z = z_ref[...].astype(F32)
    g = g * (z * _sigmoid(z))
    g = g * lax.rsqrt(jnp.mean(g * g, axis=-1, keepdims=True) + LN_EPS) * nw_ref[...]
    y_ssd = jnp.dot(g.astype(BF16), wso_ref[...], preferred_element_type=F32)
    y_att = jnp.dot(o_ref[...], wao_ref[...], preferred_element_type=F32)
    merged = (_sigmoid(gs_ref[...].astype(F32)) * y_ssd
              + _sigmoid(ga_ref[...].astype(F32)) * y_att)
    mix = jnp.dot(merged.astype(BF16), wo_ref[...], preferred_element_type=F32)
    gate1 = mod_ref[2:3, :]
    x1 = _layer_norm_rows(DEEPNORM_ALPHA * x_ref[...] + (1.0 + gate1) * mix)
    x1 = x1 * lng_ref[...] + lnb_ref[...]
    x1_ref[...] = x1
    h2 = _layer_norm_rows(x1) * (1.0 + mod_ref[4:5, :]) + mod_ref[3:4, :]
    h2_ref[...] = h2.astype(BF16)
    logits = jnp.dot(h2, wr_ref[...], precision=HIGHEST, preferred_element_type=F32) + br_ref[...]
    info, counts = _route(logits)
    rinfo_ref[...] = info
    cnt_ref[0] = jnp.broadcast_to(counts, (8, LANES))


def _mixout(y, z, o, gs, ga, x2, mod_l, norm_w, w_ssd_o, w_att_o, w_o, ln_g, ln_b, w_r, b_r,
            seq, tm):
    t, d = x2.shape
    tiles_per_batch = seq // tm
    row = lambda n: pl.BlockSpec((tm, n), lambda i: (i, 0))
    return pl.pallas_call(
        _mixout_kernel,
        grid=(t // tm,),
        in_specs=[
            row(SSD_INNER), row(SSD_INNER), row(ATT_INNER), row(d), row(d), row(d),
            pl.BlockSpec((None, N_MOD, d), lambda i: (i // tiles_per_batch, 0, 0)),
            _const_spec(norm_w.shape), _const_spec(w_ssd_o.shape), _const_spec(w_att_o.shape),
            _const_spec(w_o.shape), _const_spec(ln_g.shape), _const_spec(ln_b.shape),
            _const_spec(w_r.shape), _const_spec(b_r.shape),
        ],
        out_specs=[row(d), row(d), row(LANES), pl.BlockSpec((1, 8, LANES), lambda i: (i, 0, 0))],
        out_shape=[
            jax.ShapeDtypeStruct((t, d), F32),
            jax.ShapeDtypeStruct((t, d), BF16),
            jax.ShapeDtypeStruct((t, LANES), F32),
            jax.ShapeDtypeStruct((t // tm, 8, LANES), F32),
        ],
        compiler_params=pltpu.CompilerParams(
            dimension_semantics=("arbitrary",), vmem_limit_bytes=VMEM_LIMIT),
        name="mixer_out",
    )(y, z, o, gs, ga, x2, mod_l, norm_w, w_ssd_o, w_att_o, w_o, ln_g, ln_b, w_r, b_r)


MOE_TILE = 512
GRAN = 16
MAX_GRAN = 2 * MOE_TILE // GRAN + N_EXPERTS - 1
SORT_ROWS = (MAX_GRAN + 1) * GRAN
EXP_TM = 512


def _moe_rows(t):
    rows = 2 * t + (t // MOE_TILE) * N_EXPERTS * (GRAN - 1) + N_EXPERTS * EXP_TM
    return -(-rows // EXP_TM) * EXP_TM


def _moe_plan(cnt, t):
    cnt = cnt.astype(jnp.int32)
    ng = (cnt + GRAN - 1) // GRAN
    gl = jnp.cumsum(ng, axis=1) - ng
    ngtot = jnp.sum(ng, axis=1)
    tot_g = jnp.sum(ng, axis=0)
    tm_g = EXP_TM // GRAN
    region_g = -(-tot_g // tm_g) * tm_g
    goff_g = jnp.cumsum(region_g) - region_g
    run_g = goff_g[None, :] + jnp.cumsum(ng, axis=0) - ng
    g_idx = jnp.arange(MAX_GRAN + 1, dtype=jnp.int32)
    owner = jnp.sum(g_idx[None, :, None] >= (gl + ng)[:, None, :], axis=2)
    owner = jnp.minimum(owner, N_EXPERTS - 1).astype(jnp.int32)
    dst_g = (jnp.take_along_axis(run_g, owner, axis=1) + g_idx[None, :]
             - jnp.take_along_axis(gl, owner, axis=1))
    gdst = (dst_g * GRAN).astype(jnp.int32).reshape(-1)
    n_et = _moe_rows(t) // EXP_TM
    tstart_g = jnp.arange(n_et, dtype=jnp.int32) * tm_g
    te = jnp.sum(tstart_g[:, None] >= (goff_g + region_g)[None, :], axis=1)
    tec = jnp.minimum(te, N_EXPERTS - 1).astype(jnp.int32)
    valid = jnp.clip((goff_g[tec] + tot_g[tec] - tstart_g) * GRAN, 0, EXP_TM)
    valid = jnp.where(te >= N_EXPERTS, 0, valid).astype(jnp.int32)
    locoff = jnp.broadcast_to((gl * GRAN).astype(F32)[:, :, None],
                              (cnt.shape[0], N_EXPERTS, MOE_TILE))
    return ngtot.astype(jnp.int32), gdst, tec, valid, locoff


def _sorted_slots(rinfo_ref, locoff_ref, upper_ref):
    rt = jnp.transpose(rinfo_ref[...])
    i1, i2, w1, w2 = rt[0:1, :], rt[1:2, :], rt[2:3, :], rt[3:4, :]
    e_iota = lax.broadcasted_iota(jnp.int32, (N_EXPERTS, MOE_TILE), 0).astype(F32)
    a1 = e_iota == i1
    a2 = e_iota == i2
    a_t = jnp.where(a1, 1.0, jnp.where(a2, 1.0, 0.0)).astype(BF16)
    rank = jnp.dot(a_t, upper_ref[...], preferred_element_type=F32)
    pos = locoff_ref[0] + rank
    d1 = jnp.sum(jnp.where(a1, pos, 0.0), axis=0, keepdims=True)
    d2 = jnp.sum(jnp.where(a2, pos, 0.0), axis=0, keepdims=True)
    r_iota = lax.broadcasted_iota(jnp.int32, (SORT_ROWS, MOE_TILE), 0).astype(F32)
    return r_iota, d1, d2, w1, w2


def _granule_copy(src, dst, sem):
    return pltpu.make_async_copy(src, dst, sem)


def _dispatch_kernel(ngtot_s, gdst_s, h_ref, rinfo_ref, locoff_ref, upper_ref, xs_init, xs_hbm,
                     buf, sem):
    del xs_init
    i = pl.program_id(0)
    last = pl.num_programs(0) - 1
    slot = lax.rem(i, 2)

    def copy(tile, g, sl):
        src = buf.at[sl, pl.ds(pl.multiple_of(g * GRAN, GRAN), GRAN), :]
        row = pl.multiple_of(gdst_s[tile * (MAX_GRAN + 1) + g], GRAN)
        return _granule_copy(src, xs_hbm.at[pl.ds(row, GRAN), :], sem.at[sl])

    def wait_tile(tile, sl):
        def body(g, c):
            copy(tile, g, sl).wait()
            return c
        lax.fori_loop(0, ngtot_s[tile], body, 0)

    @pl.when(i >= 2)
    def _():
        wait_tile(i - 2, slot)

    r_iota, d1, d2, _, _ = _sorted_slots(rinfo_ref, locoff_ref, upper_ref)
    sel = jnp.where(r_iota == d1, 1.0, jnp.where(r_iota == d2, 1.0, 0.0)).astype(BF16)
    buf[slot] = jnp.dot(sel, h_ref[...], preferred_element_type=F32).astype(BF16)

    def start(g, c):
        copy(i, g, slot).start()
        return c
    lax.fori_loop(0, ngtot_s[i], start, 0)

    @pl.when(i == last)
    def _():
        @pl.when(i >= 1)
        def _():
            wait_tile(i - 1, 1 - slot)
        wait_tile(i, slot)


def _dispatch(plan, h2, rinfo, upper):
    ngtot, gdst, _, _, locoff = plan
    t, d = h2.shape
    n_tiles = t // MOE_TILE
    grid_spec = pltpu.PrefetchScalarGridSpec(
        num_scalar_prefetch=2,
        grid=(n_tiles,),
        in_specs=[
            pl.BlockSpec((MOE_TILE, d), lambda i, *_: (i, 0)),
            pl.BlockSpec((MOE_TILE, LANES), lambda i, *_: (i, 0)),
            pl.BlockSpec((1, N_EXPERTS, MOE_TILE), lambda i, *_: (i, 0, 0)),
            pl.BlockSpec((MOE_TILE, MOE_TILE), lambda i, *_: (0, 0)),
            pl.BlockSpec(memory_space=pl.ANY),
        ],
        out_specs=pl.BlockSpec(memory_space=pl.ANY),
        scratch_shapes=[pltpu.VMEM((2, SORT_ROWS, d), BF16), pltpu.SemaphoreType.DMA((2,))],
    )
    xs_init = jnp.zeros((_moe_rows(t), d), BF16)
    return pl.pallas_call(
        _dispatch_kernel,
        grid_spec=grid_spec,
        out_shape=jax.ShapeDtypeStruct((_moe_rows(t), d), BF16),
        input_output_aliases={6: 0},
        compiler_params=pltpu.CompilerParams(
            dimension_semantics=("arbitrary",), vmem_limit_bytes=VMEM_LIMIT),
        name="moe_dispatch",
    )(ngtot, gdst, h2, rinfo, locoff, upper, xs_init)


def _expert_kernel(te_s, valid_s, x_ref, wg_ref, wu_ref, wd_ref, y_ref):
    valid = valid_s[pl.program_id(0)]

    @pl.when(valid > 0)
    def _():
        x = x_ref[...]
        a = jnp.dot(x, wg_ref[0], preferred_element_type=F32)
        u = jnp.dot(x, wu_ref[0], preferred_element_type=F32)
        hid = ((a * _sigmoid(a)) * u).astype(BF16)
        y_ref[...] = jnp.dot(hid, wd_ref[0], preferred_element_type=F32).astype(y_ref.dtype)

    @pl.when(valid <= 0)
    def _():
        y_ref[...] = jnp.zeros_like(y_ref)


def _experts(plan, xs, w_gate, w_up, w_down):
    _, _, tec, valid, _ = plan
    rows, d = xs.shape
    _, _, ff = w_gate.shape
    grid_spec = pltpu.PrefetchScalarGridSpec(
        num_scalar_prefetch=2,
        grid=(rows // EXP_TM,),
        in_specs=[
            pl.BlockSpec((EXP_TM, d), lambda i, te, va: (i, 0)),
            pl.BlockSpec((1, d, ff), lambda i, te, va: (te[i], 0, 0)),
            pl.BlockSpec((1, d, ff), lambda i, te, va: (te[i], 0, 0)),
            pl.BlockSpec((1, ff, d), lambda i, te, va: (te[i], 0, 0)),
        ],
        out_specs=pl.BlockSpec((EXP_TM, d), lambda i, te, va: (i, 0)),
    )
    return pl.pallas_call(
        _expert_kernel,
        grid_spec=grid_spec,
        out_shape=jax.ShapeDtypeStruct((rows, d), BF16),
        compiler_params=pltpu.CompilerParams(
            dimension_semantics=("arbitrary",), vmem_limit_bytes=VMEM_LIMIT),
        name="moe_experts",
    )(tec, valid, xs, w_gate, w_up, w_down)


def _combine_kernel(ngtot_s, gdst_s, ys_hbm, rinfo_ref, locoff_ref, upper_ref, x1_ref, mod_ref,
                    lng_ref, lnb_ref, o_ref, buf, sem):
    i = pl.program_id(0)
    last = pl.num_programs(0) - 1
    slot = lax.rem(i, 2)

    def copy(tile, g, sl):
        row = pl.multiple_of(gdst_s[tile * (MAX_GRAN + 1) + g], GRAN)
        dst = buf.at[sl, pl.ds(pl.multiple_of(g * GRAN, GRAN), GRAN), :]
        return _granule_copy(ys_hbm.at[pl.ds(row, GRAN), :], dst, sem.at[sl])

    def start_tile(tile, sl):
        def body(g, c):
            copy(tile, g, sl).start()
            return c
        lax.fori_loop(0, ngtot_s[tile], body, 0)

    @pl.when(i == 0)
    def _():
        buf[...] = jnp.zeros_like(buf)
        start_tile(0, 0)

    @pl.when(i < last)
    def _():
        start_tile(i + 1, 1 - slot)

    def wait(g, c):
        copy(i, g, slot).wait()
        return c
    lax.fori_loop(0, ngtot_s[i], wait, 0)

    r_iota, d1, d2, w1, w2 = _sorted_slots(rinfo_ref, locoff_ref, upper_ref)
    wsel = jnp.where(r_iota == d1, w1, jnp.where(r_iota == d2, w2, 0.0)).astype(BF16)
    moe = lax.dot_general(wsel, buf[slot], (((0,), (0,)), ((), ())),
                          preferred_element_type=F32)
    gate2 = mod_ref[5:6, :]
    xn = _layer_norm_rows(DEEPNORM_ALPHA * x1_ref[...] + (1.0 + gate2) * moe)
    o_ref[...] = xn * lng_ref[...] + lnb_ref[...]


def _combine(plan, ys, rinfo, upper, x1, mod_l, ln_g, ln_b, seq):
    ngtot, gdst, _, _, locoff = plan
    t, d = x1.shape
    tiles_per_batch = seq // MOE_TILE
    grid_spec = pltpu.PrefetchScalarGridSpec(
        num_scalar_prefetch=2,
        grid=(t // MOE_TILE,),
        in_specs=[
            pl.BlockSpec(memory_space=pl.ANY),
            pl.BlockSpec((MOE_TILE, LANES), lambda i, *_: (i, 0)),
            pl.BlockSpec((1, N_EXPERTS, MOE_TILE), lambda i, *_: (i, 0, 0)),
            pl.BlockSpec((MOE_TILE, MOE_TILE), lambda i, *_: (0, 0)),
            pl.BlockSpec((MOE_TILE, d), lambda i, *_: (i, 0)),
            pl.BlockSpec((None, N_MOD, d), lambda i, *_: (i // tiles_per_batch, 0, 0)),
            pl.BlockSpec((1, d), lambda i, *_: (0, 0)),
            pl.BlockSpec((1, d), lambda i, *_: (0, 0)),
        ],
        out_specs=pl.BlockSpec((MOE_TILE, d), lambda i, *_: (i, 0)),
        scratch_shapes=[pltpu.VMEM((2, SORT_ROWS, d), BF16), pltpu.SemaphoreType.DMA((2,))],
    )
    return pl.pallas_call(
        _combine_kernel,
        grid_spec=grid_spec,
        out_shape=jax.ShapeDtypeStruct((t, d), F32),
        compiler_params=pltpu.CompilerParams(
            dimension_semantics=("arbitrary",), vmem_limit_bytes=VMEM_LIMIT),
        name="moe_combine",
    )(ngtot, gdst, ys, rinfo, locoff, upper, x1, mod_l, ln_g, ln_b)


def _pad_lanes(a):
    return jnp.pad(a, ((0, 0), (0, LANES - a.shape[1])))


def kernel(x, c, w_mod, b_mod, w_in, conv_w, conv_b, dt_bias, a_log, d_skip, ssd_norm_w, forget_b,
           w_ssd_o, w_att_o, w_o, ln1_g, ln1_b, w_router_group, b_router_group, w_router_expert,
           b_router_expert, w_gate, w_up, w_down, ln2_g, ln2_b):
    bsz, seq, d = x.shape
    depth = w_mod.shape[0]
    t = bsz * seq
    assert seq % MOE_TILE == 0 and d == SSD_INNER
    tm_proj = min(512, seq)
    ssd_rows = min(512, seq)
    upper = jnp.asarray(np.triu(np.ones((MOE_TILE, MOE_TILE), np.float32), k=1), BF16)
    tq = min(512, seq)
    tk = min(512, seq)

    mod = _modulation(c, w_mod, b_mod).reshape(depth, bsz, N_MOD, d)
    consts = _ssd_constants()
    placement = _attn_placement()
    in_sizes = (SSD_INNER, SSD_CONV_CH, SSD_HEADS, ATT_INNER, ATT_INNER, ATT_INNER, ATT_HEADS, d, d)
    offs = np.concatenate([[0], np.cumsum(in_sizes)]).tolist()

    x2 = x.reshape(t, d)
    for l in range(depth):
        cols = [w_in[l][:, offs[i]:offs[i + 1]] for i in range(len(in_sizes))]
        wz, wxbc, wdt, wq, wk, wv, wf, wgs, wga = cols
        wq = wq * (math.log2(math.e) * ATT_HEAD_DIM ** -0.5)
        big = [w.astype(BF16) for w in (wz, wxbc, wq, wk, wv, wgs, wga)]
        w_small = _pad_lanes(jnp.concatenate([wdt, wf], axis=1))
        z, xbc, q, k, v, gs, ga, small = _inproj(x2, mod[l], big, w_small, seq, tm_proj)

        hvec = jnp.concatenate([
            _pad_lanes(jnp.concatenate([dt_bias[l], forget_b[l]])[None, :]),
            _pad_lanes(a_log[l][None, :]),
            jnp.zeros((6, LANES), F32)], axis=0)
        dskip_x = jnp.repeat(d_skip[l], SSD_HEAD_DIM)[None, :]
        y, fcum = _ssd(xbc.reshape(bsz, seq, SSD_CONV_CH), small.reshape(bsz, seq, LANES),
                       conv_w[l], conv_b[l][None, :], hvec, dskip_x, consts, ssd_rows)

        o = _attention(q.reshape(bsz, seq, ATT_INNER), k.reshape(bsz, seq, ATT_INNER),
                       v.reshape(bsz, seq, ATT_INNER), fcum, placement, tq, tk)

        w_r = _pad_lanes(jnp.concatenate([w_router_expert[l], w_router_group[l]], axis=1))
        b_r = _pad_lanes(jnp.concatenate([b_router_expert[l], b_router_group[l]])[None, :])
        x1, h2, rinfo, cnt = _mixout(
            y.reshape(t, SSD_INNER), z, o.reshape(t, ATT_INNER), gs, ga, x2, mod[l],
            ssd_norm_w[l][None, :], w_ssd_o[l].astype(BF16), w_att_o[l].astype(BF16),
            w_o[l].astype(BF16), ln1_g[l][None, :], ln1_b[l][None, :], w_r, b_r, seq, MOE_TILE)

        plan = _moe_plan(cnt[:, 0, :N_EXPERTS], t)
        xs = _dispatch(plan, h2, rinfo, upper)
        ys = _experts(plan, xs, w_gate[l].astype(BF16), w_up[l].astype(BF16),
                      w_down[l].astype(BF16))
        x2 = _combine(plan, ys, rinfo, upper, x1, mod[l], ln2_g[l][None, :], ln2_b[l][None, :],
                      seq)
    return x2.reshape(bsz, seq, d)
```

```python
import functools
import math

import numpy as np
import jax
import jax.numpy as jnp
from jax import lax
from jax.experimental import pallas as pl
from jax.experimental.pallas import tpu as pltpu

SSD_HEAD_DIM = 64
SSD_HEADS = 16
SSD_GROUPS = 4
SSD_STATE = 128
SSD_CHUNK = 128
SSD_CONV = 4
SSD_INNER = SSD_HEADS * SSD_HEAD_DIM
SSD_BC = SSD_GROUPS * SSD_STATE
SSD_CONV_CH = SSD_INNER + 2 * SSD_BC
ATT_HEAD_DIM = 64
ATT_HEADS = 8
ATT_INNER = ATT_HEADS * ATT_HEAD_DIM
N_GROUPS = 4
EXPERTS_PER_GROUP = 4
N_EXPERTS = 16
N_MOD = 6
DEPTH_FOR_DEEPNORM = 4
DEEPNORM_ALPHA = (2 * DEPTH_FOR_DEEPNORM) ** 0.25
LN_EPS = 1e-5

LANES = 128
VMEM_LIMIT = 56 * 1024 * 1024

F32 = jnp.float32
BF16 = jnp.bfloat16
HIGHEST = lax.Precision.HIGHEST


def _sigmoid(x):
    return 1.0 / (1.0 + jnp.exp2(x * (-math.log2(math.e))))


def _layer_norm_rows(x):
    mu = jnp.mean(x, axis=-1, keepdims=True)
    xc = x - mu
    var = jnp.mean(xc * xc, axis=-1, keepdims=True)
    return xc * lax.rsqrt(var + LN_EPS)


def _dot_split(h, h_hi, w):
    h_lo = (h - h_hi.astype(F32)).astype(BF16)
    w_hi = w.astype(BF16)
    w_lo = (w - w_hi.astype(F32)).astype(BF16)
    r = jnp.dot(h_hi, jnp.concatenate([w_hi, w_lo], axis=1), preferred_element_type=F32)
    return r[:, :LANES] + r[:, LANES:] + jnp.dot(h_lo, w_hi, preferred_element_type=F32)


def _const_spec(shape):
    n = len(shape)
    return pl.BlockSpec(shape, lambda *_: (0,) * n)


def _mod_kernel(c_ref, w_ref, b_ref, o_ref):
    c = c_ref[...]
    ca = c * _sigmoid(c)
    o_ref[0] = jnp.dot(ca, w_ref[0], precision=HIGHEST, preferred_element_type=F32) + b_ref[0]


def _modulation(c, w_mod, b_mod):
    depth, d, n = w_mod.shape
    bsz = c.shape[0]
    tn = 1024
    return pl.pallas_call(
        _mod_kernel,
        grid=(depth, n // tn),
        in_specs=[
            pl.BlockSpec((bsz, d), lambda l, j: (0, 0)),
            pl.BlockSpec((1, d, tn), lambda l, j: (l, 0, j)),
            pl.BlockSpec((1, 1, tn), lambda l, j: (l, 0, j)),
        ],
        out_specs=pl.BlockSpec((1, bsz, tn), lambda l, j: (l, 0, j)),
        out_shape=jax.ShapeDtypeStruct((depth, bsz, n), F32),
        compiler_params=pltpu.CompilerParams(
            dimension_semantics=("arbitrary", "arbitrary"), vmem_limit_bytes=VMEM_LIMIT),
        name="modulation",
    )(c, w_mod, b_mod.reshape(depth, 1, n))


def _inproj_kernel(x_ref, mod_ref, wz_ref, wxbc_ref, wq_ref, wk_ref, wv_ref, wgs_ref, wga_ref,
                   wsm_ref, z_ref, xbc_ref, q_ref, k_ref, v_ref, gs_ref, ga_ref, sm_ref):
    x = x_ref[...]
    shift = mod_ref[0:1, :]
    scale = mod_ref[1:2, :]
    h = _layer_norm_rows(x) * (1.0 + scale) + shift
    hb = h.astype(BF16)
    for w_ref, o_ref in ((wz_ref, z_ref), (wxbc_ref, xbc_ref), (wq_ref, q_ref), (wk_ref, k_ref),
                         (wv_ref, v_ref), (wgs_ref, gs_ref), (wga_ref, ga_ref)):
        o_ref[...] = jnp.dot(hb, w_ref[...], preferred_element_type=F32).astype(o_ref.dtype)
    sm_ref[...] = _dot_split(h, hb, wsm_ref[...])


def _inproj(x2, mod_l, weights, w_small, seq, tm):
    t, d = x2.shape
    tiles_per_batch = seq // tm
    widths = [w.shape[1] for w in weights]
    in_specs = [
        pl.BlockSpec((tm, d), lambda i: (i, 0)),
        pl.BlockSpec((None, N_MOD, d), lambda i: (i // tiles_per_batch, 0, 0)),
    ]
    in_specs += [_const_spec(w.shape) for w in weights]
    in_specs += [_const_spec(w_small.shape)]
    out_specs = [pl.BlockSpec((tm, n), lambda i: (i, 0)) for n in widths]
    out_specs += [pl.BlockSpec((tm, LANES), lambda i: (i, 0))]
    out_shape = [jax.ShapeDtypeStruct((t, n), BF16) for n in widths]
    out_shape += [jax.ShapeDtypeStruct((t, LANES), F32)]
    return pl.pallas_call(
        _inproj_kernel,
        grid=(t // tm,),
        in_specs=in_specs,
        out_specs=out_specs,
        out_shape=out_shape,
        compiler_params=pltpu.CompilerParams(
            dimension_semantics=("arbitrary",), vmem_limit_bytes=VMEM_LIMIT),
        name="inproj",
    )(x2, mod_l, *weights, w_small)


CONV_HIST = 16
CONV_WIN = CONV_HIST + SSD_CHUNK


def _ssd_constants():
    L = SSD_CHUNK
    sh = np.zeros((L, SSD_CONV * CONV_WIN), np.float32)
    for j in range(SSD_CONV):
        for t in range(L):
            sh[t, j * CONV_WIN + CONV_HIST + t - j] = 1.0
    tril = np.tril(np.ones((L, L), np.float32))
    e3 = np.zeros((LANES, SSD_INNER), np.float32)
    for piece in range(3):
        for h in range(SSD_HEADS):
            e3[32 * piece + h, h * SSD_HEAD_DIM:(h + 1) * SSD_HEAD_DIM] = 1.0
    r = SSD_HEADS // SSD_GROUPS
    bd = np.zeros((r * L, r * SSD_HEAD_DIM), np.float32)
    for i in range(r):
        bd[i * L:(i + 1) * L, i * SSD_HEAD_DIM:(i + 1) * SSD_HEAD_DIM] = 1.0
    return (jnp.asarray(sh, BF16), jnp.asarray(tril, BF16), jnp.asarray(e3, BF16),
            jnp.asarray(bd, BF16))


def _split3_bf16(a):
    hi = a.astype(BF16)
    r1 = a - hi.astype(F32)
    mid = r1.astype(BF16)
    lo = (r1 - mid.astype(F32)).astype(BF16)
    return hi, mid, lo


def _expand_heads(xm, e3):
    hi, mid, lo = _split3_bf16(xm)
    packed = (hi.astype(F32) + pltpu.roll(mid.astype(F32), 32, 1)
              + pltpu.roll(lo.astype(F32), 64, 1)).astype(BF16)
    return jnp.dot(packed, e3, preferred_element_type=F32)


def _ssd_kernel(xbc_ref, sm_ref, cw_ref, cb_ref, hv_ref, dskip_ref, sh_ref, tril_ref, e3_ref,
                bd_ref, y_ref, fcum_ref, uext, state, fcarry, *, rows):
    L = SSD_CHUNK
    j = pl.program_id(1)

    @pl.when(j == 0)
    def _():
        uext[0:CONV_HIST, :] = jnp.zeros((CONV_HIST, SSD_CONV_CH), BF16)
        state[...] = jnp.zeros_like(state)
        fcarry[...] = jnp.zeros_like(fcarry)

    uext[CONV_HIST:CONV_HIST + rows, :] = xbc_ref[0]
    cw_b = cw_ref[...].astype(BF16)

    lane = lax.broadcasted_iota(jnp.int32, (1, LANES), 1)
    is_dt = lane < SSD_HEADS
    is_f = jnp.logical_and(lane >= SSD_HEADS, lane < SSD_HEADS + ATT_HEADS)
    a_neg = -jnp.exp(hv_ref[1:2, :])
    bias = hv_ref[0:1, :]
    rr = lax.broadcasted_iota(jnp.int32, (L, L), 0)
    cc = lax.broadcasted_iota(jnp.int32, (L, L), 1)
    causal = rr >= cc
    gw = (SSD_HEADS // SSD_GROUPS) * SSD_HEAD_DIM

    for c in range(rows // L):
        r0 = c * L
        win = uext[r0:r0 + CONV_WIN, :]
        taps = jnp.concatenate([win * cw_b[SSD_CONV - 1 - j:SSD_CONV - j, :]
                                for j in range(SSD_CONV)], axis=0)
        conv = jnp.dot(sh_ref[...], taps, preferred_element_type=F32) + cb_ref[...]
        act = conv * _sigmoid(conv)
        xs = act[:, :SSD_INNER]
        bm = act[:, SSD_INNER:SSD_INNER + SSD_BC]
        cm = act[:, SSD_INNER + SSD_BC:]

        pre = sm_ref[0, r0:r0 + L, :] + bias
        tail = jnp.log(1.0 + jnp.exp(-jnp.abs(pre)))
        dt = jnp.where(is_dt, jnp.maximum(pre, 0.0) + tail, 0.0)
        log_f = jnp.minimum(pre, 0.0) - tail
        comb = jnp.where(is_dt, dt * a_neg, jnp.where(is_f, log_f, 0.0))
        c3 = jnp.dot(tril_ref[...], jnp.concatenate(_split3_bf16(comb), axis=1),
                     preferred_element_type=F32)
        cs = c3[:, :LANES] + c3[:, LANES:2 * LANES] + c3[:, 2 * LANES:]
        fc = cs + fcarry[...]
        fcarry[...] = jnp.where(is_f, fc[L - 1:L, :], 0.0)
        f_hi, f_mid, f_lo = _split3_bf16(jnp.where(is_f, fc * math.log2(math.e), 0.0))
        fcum_ref[0, r0:r0 + L, :] = (
            f_hi.astype(F32) + pltpu.roll(f_mid.astype(F32), 32, 1)
            + pltpu.roll(f_lo.astype(F32), 64, 1)
            + jnp.where(lane == LANES - 1, 1.0, 0.0)).astype(BF16)

        a_cs = jnp.where(is_dt, cs, 0.0)
        a_cs_x = _expand_heads(a_cs, e3_ref[...])
        dt_x = _expand_heads(dt, e3_ref[...])
        a_last_x = a_cs_x[L - 1:L, :]
        xdt = xs * dt_x
        xdt_b = xdt.astype(BF16)
        xend_b = (xdt * jnp.exp(a_last_x - a_cs_x)).astype(BF16)
        e_acs = jnp.exp(a_cs_x)
        e_last = jnp.exp(a_last_x)
        a_cs_t = jnp.transpose(a_cs)

        y_groups = []
        for g in range(SSD_GROUPS):
            bg = bm[:, g * SSD_STATE:(g + 1) * SSD_STATE]
            cg_b = cm[:, g * SSD_STATE:(g + 1) * SSD_STATE].astype(BF16)
            cb = lax.dot_general(cg_b, bg.astype(BF16), (((1,), (1,)), ((), ())),
                                 preferred_element_type=F32)
            ms = []
            for r in range(SSD_HEADS // SSD_GROUPS):
                h = g * (SSD_HEADS // SSD_GROUPS) + r
                seg = a_cs[:, h:h + 1] - a_cs_t[h:h + 1, :]
                dec = jnp.exp(jnp.where(causal, seg, -jnp.inf))
                ms.append((cb * dec).astype(BF16))
            m_cat = jnp.concatenate(ms, axis=1)
            xg = xdt_b[:, g * gw:(g + 1) * gw]
            x_bd = jnp.concatenate([xg] * (SSD_HEADS // SSD_GROUPS), axis=0) * bd_ref[...]
            y_diag = jnp.dot(m_cat, x_bd, preferred_element_type=F32)
            st = state[:, g * gw:(g + 1) * gw]
            y_off = jnp.dot(cg_b, st.astype(BF16), preferred_element_type=F32)
            y_groups.append(y_diag + y_off * e_acs[:, g * gw:(g + 1) * gw])
            new_st = jnp.dot(jnp.transpose(bg).astype(BF16), xend_b[:, g * gw:(g + 1) * gw],
                             preferred_element_type=F32)
            state[:, g * gw:(g + 1) * gw] = st * e_last[:, g * gw:(g + 1) * gw] + new_st
        y = jnp.concatenate(y_groups, axis=1) + dskip_ref[...] * xs
        y_ref[0, r0:r0 + L, :] = y.astype(y_ref.dtype)

    uext[0:CONV_HIST, :] = uext[rows:rows + CONV_HIST, :]


def _ssd(xbc, small, conv_w, conv_b, hvec, dskip_x, consts, rows):
    bsz, seq, _ = xbc.shape
    sh, tril, e3, bd = consts
    kern = functools.partial(_ssd_kernel, rows=rows)
    return pl.pallas_call(
        kern,
        grid=(bsz, seq // rows),
        in_specs=[
            pl.BlockSpec((1, rows, SSD_CONV_CH), lambda b, j: (b, j, 0)),
            pl.BlockSpec((1, rows, LANES), lambda b, j: (b, j, 0)),
            _const_spec(conv_w.shape), _const_spec(conv_b.shape), _const_spec(hvec.shape),
            _const_spec(dskip_x.shape), _const_spec(sh.shape), _const_spec(tril.shape),
            _const_spec(e3.shape), _const_spec(bd.shape),
        ],
        out_specs=[
            pl.BlockSpec((1, rows, SSD_INNER), lambda b, j: (b, j, 0)),
            pl.BlockSpec((1, rows, LANES), lambda b, j: (b, j, 0)),
        ],
        out_shape=[
            jax.ShapeDtypeStruct((bsz, seq, SSD_INNER), BF16),
            jax.ShapeDtypeStruct((bsz, seq, LANES), BF16),
        ],
        scratch_shapes=[
            pltpu.VMEM((CONV_HIST + rows, SSD_CONV_CH), BF16),
            pltpu.VMEM((SSD_STATE, SSD_INNER), F32),
            pltpu.VMEM((1, LANES), F32),
        ],
        compiler_params=pltpu.CompilerParams(
            dimension_semantics=("arbitrary", "arbitrary"), vmem_limit_bytes=VMEM_LIMIT),
        name="ssd",
    )(xbc, small, conv_w, conv_b, hvec, dskip_x, sh, tril, e3, bd)


def _attn_placement():
    pq = np.zeros((ATT_HEADS // 2, LANES, 2 * LANES), np.float32)
    pk = np.zeros_like(pq)
    for p in range(ATT_HEADS // 2):
        for hd in range(2):
            h = 2 * p + hd
            col = hd * LANES + (ATT_HEAD_DIM if hd == 0 else 0)
            for piece in range(3):
                pq[p, 32 * piece + SSD_HEADS + h, col + piece] = 1.0
                pq[p, LANES - 1, col + 3 + piece] = 1.0
                pk[p, LANES - 1, col + piece] = 1.0
                pk[p, 32 * piece + SSD_HEADS + h, col + 3 + piece] = -1.0
    return jnp.asarray(pq, BF16), jnp.asarray(pk, BF16)


def _attn_kernel(q_ref, k_ref, v_ref, fq_ref, fk_ref, pq_ref, pk_ref, o_ref, kaug_sc, s_sc,
                 *, tq, tk):
    qi = pl.program_id(2)
    seq = k_ref.shape[1]
    r = tq // tk
    n_full = qi * r
    key_idx = lax.broadcasted_iota(jnp.int32, (tk, tq), 0)
    qry_idx = lax.broadcasted_iota(jnp.int32, (tk, tq), 1)
    lane2 = lax.broadcasted_iota(jnp.int32, (1, 2 * LANES), 1)
    keep = jnp.logical_or(lane2 < ATT_HEAD_DIM, lane2 >= LANES + ATT_HEAD_DIM)

    def augment(x, f, place):
        aug = jnp.dot(f, place, preferred_element_type=F32).astype(BF16)
        return jnp.where(keep, jnp.concatenate([x, x], axis=1), aug)

    npair = q_ref.shape[2] // LANES
    heads = range(2 * npair)
    pair_cols = lambda pp: slice(pp * LANES, (pp + 1) * LANES)
    slab_cols = lambda h: slice(h * LANES, (h + 1) * LANES)

    @pl.when(qi == 0)
    def _():
        def fill(c, carry):
            rows = pl.ds(pl.multiple_of(c * tk, tk), tk)
            for pp in range(npair):
                kaug_sc[rows, pp * 2 * LANES:(pp + 1) * 2 * LANES] = augment(
                    k_ref[0, rows, pair_cols(pp)], fk_ref[0, rows, :], pk_ref[pp])
            return carry
        lax.fori_loop(0, seq // tk, fill, 0)

    q_aug = [augment(q_ref[0, :, pair_cols(pp)], fq_ref[0], pq_ref[pp])
             for pp in range(npair)]

    def scores(kblk):
        start = pl.multiple_of(kblk * tk, tk)
        kb = kaug_sc[pl.ds(start, tk), :]
        return [lax.dot_general(kb[:, slab_cols(h)], q_aug[h // 2][:, slab_cols(h % 2)],
                                (((1,), (1,)), ((), ())), preferred_element_type=F32)
                for h in heads]

    def prepare(s_list, diag):
        mcols = []
        for h in heads:
            s_t = s_list[h]
            if diag is not None:
                s_t = jnp.where(key_idx + diag * tk <= qry_idx, s_t, -jnp.inf)
            s_sc[h] = s_t
            mcols.append(jnp.max(s_t, axis=0, keepdims=True))
        return mcols

    def consume(kblk, mcols, carry):
        start = pl.multiple_of(kblk * tk, tk)
        vb = v_ref[0, pl.ds(start, tk), :]
        new = []
        for h in heads:
            m_old, acc = carry[h]
            m_new = jnp.maximum(m_old, mcols[h])
            p_t = jnp.exp2((s_sc[h] - m_new).astype(BF16))
            alpha = jnp.exp2(m_old - m_new)
            vb_p = vb[:, pair_cols(h // 2)]
            vb_h = jnp.where(own_lanes[h % 2], vb_p, jnp.ones_like(vb_p))
            pv = lax.dot_general(vb_h, p_t, (((0,), (0,)), ((), ())),
                                 preferred_element_type=F32)
            new.append((m_new, alpha * acc + pv))
        return tuple(new)

    lane1 = lax.broadcasted_iota(jnp.int32, (1, LANES), 1)
    own_lanes = (lane1 < ATT_HEAD_DIM, lane1 >= ATT_HEAD_DIM)
    carry = tuple((jnp.full((1, tq), -jnp.inf, F32), jnp.zeros((LANES, tq), F32))
                  for _ in heads)
    mcols = prepare(scores(n_full), 0)
    for d in range(r):
        s_next = scores(n_full + d + 1 if d + 1 < r else 0)
        carry = consume(n_full + d, mcols, carry)
        mcols = prepare(s_next, d + 1 if d + 1 < r else None)

    def body(jb, state):
        carry, mcols = state
        s_next = scores(jnp.minimum(jb + 1, n_full - 1))
        carry = consume(jb, mcols, carry)
        return carry, tuple(prepare(s_next, None))

    carry, _ = lax.fori_loop(0, n_full, body, (carry, tuple(mcols)))
    hdim = ATT_HEAD_DIM
    for pp in range(npair):
        acc0, acc1 = carry[2 * pp][1], carry[2 * pp + 1][1]
        o_t = jnp.concatenate([acc0[:hdim] / acc0[hdim:hdim + 1], acc1[hdim:] / acc1[0:1]],
                              axis=0)
        o_ref[0, :, pair_cols(pp)] = jnp.transpose(o_t).astype(o_ref.dtype)


ATT_PAIRS_PER_STEP = 2


def _attention(q, k, v, f_pieces, placement, tq, tk):
    bsz, seq, _ = q.shape
    npair = ATT_PAIRS_PER_STEP
    steps = ATT_HEADS // (2 * npair)
    pq, pk = placement
    kern = functools.partial(_attn_kernel, tq=tq, tk=tk)
    wide = npair * LANES
    return pl.pallas_call(
        kern,
        grid=(bsz, steps, seq // tq),
        in_specs=[
            pl.BlockSpec((1, tq, wide), lambda b, p, i: (b, i, p)),
            pl.BlockSpec((1, seq, wide), lambda b, p, i: (b, 0, p)),
            pl.BlockSpec((1, seq, wide), lambda b, p, i: (b, 0, p)),
            pl.BlockSpec((1, tq, LANES), lambda b, p, i: (b, i, 0)),
            pl.BlockSpec((1, seq, LANES), lambda b, p, i: (b, 0, 0)),
            pl.BlockSpec((npair, LANES, 2 * LANES), lambda b, p, i: (p, 0, 0)),
            pl.BlockSpec((npair, LANES, 2 * LANES), lambda b, p, i: (p, 0, 0)),
        ],
        out_specs=pl.BlockSpec((1, tq, wide), lambda b, p, i: (b, i, p)),
        out_shape=jax.ShapeDtypeStruct((bsz, seq, ATT_INNER), BF16),
        scratch_shapes=[pltpu.VMEM((seq, 2 * wide), BF16),
                        pltpu.VMEM((2 * npair, tk, tq), F32)],
        compiler_params=pltpu.CompilerParams(
            dimension_semantics=("arbitrary", "arbitrary", "arbitrary"),
            vmem_limit_bytes=VMEM_LIMIT),
        name="fox_attention",
    )(q, k, v, f_pieces, f_pieces, pq, pk)


def _route(logits):
    lane = lax.broadcasted_iota(jnp.int32, logits.shape, 1)
    neg = -jnp.inf
    big = jnp.int32(1 << 20)
    is_g = jnp.logical_and(lane >= N_EXPERTS, lane < N_EXPERTS + N_GROUPS)
    gl = jnp.where(is_g, logits, neg)
    gmax = jnp.max(gl, axis=1, keepdims=True)
    gsum = jnp.sum(jnp.exp(gl - gmax), axis=1, keepdims=True)
    g_p = 1.0 / gsum
    g_idx = jnp.min(jnp.where(gl == gmax, lane, big), axis=1, keepdims=True) - N_EXPERTS
    lo = g_idx * EXPERTS_PER_GROUP
    in_group = jnp.logical_and(lane >= lo, lane < lo + EXPERTS_PER_GROUP)
    el = jnp.where(in_group, logits, neg)
    m1 = jnp.max(el, axis=1, keepdims=True)
    i1 = jnp.min(jnp.where(el == m1, lane, big), axis=1, keepdims=True)
    el2 = jnp.where(lane == i1, neg, el)
    m2 = jnp.max(el2, axis=1, keepdims=True)
    i2 = jnp.min(jnp.where(el2 == m2, lane, big), axis=1, keepdims=True)
    e2 = jnp.exp(m2 - m1)
    w1 = g_p / (1.0 + e2)
    w2 = g_p * e2 / (1.0 + e2)
    info = jnp.where(lane == 0, i1.astype(F32), jnp.where(lane == 1, i2.astype(F32),
                     jnp.where(lane == 2, w1, jnp.where(lane == 3, w2, 0.0))))
    chosen = jnp.where(lane == i1, 1.0, jnp.where(lane == i2, 1.0, 0.0))
    return info, jnp.sum(chosen, axis=0, keepdims=True)


def _mixout_kernel(y_ref, z_ref, o_ref, gs_ref, ga_ref, x_ref, mod_ref, nw_ref, wso_ref, wao_ref,
                   wo_ref, lng_ref, lnb_ref, wr_ref, br_ref, x1_ref, h2_ref, rinfo_ref, cnt_ref):
    tm = x_ref.shape[0]
    parts = 2
    rows_per = tm // parts
    counts = jnp.zeros((1, LANES), F32)
    for part in range(parts):
        rs = slice(part * rows_per, (part + 1) * rows_per)
        g = y_ref[rs, :].astype(F32)
        z = z_ref[rs, :].astype(F32)
        g = g * (z * _sigmoid(z))
        g = g * lax.rsqrt(jnp.mean(g * g, axis=-1, keepdims=True) + LN_EPS) * nw_ref[...]
        y_ssd = jnp.dot(g.astype(BF16), wso_ref[...], preferred_element_type=F32)
        y_att = jnp.dot(o_ref[rs, :], wao_ref[...], preferred_element_type=F32)
        merged = (_sigmoid(gs_ref[rs, :].astype(F32)) * y_ssd
                  + _sigmoid(ga_ref[rs, :].astype(F32)) * y_att)
        mix = jnp.dot(merged.astype(BF16), wo_ref[...], preferred_element_type=F32)
        gate1 = mod_ref[2:3, :]
        x1 = _layer_norm_rows(DEEPNORM_ALPHA * x_ref[rs, :] + (1.0 + gate1) * mix)
        x1 = x1 * lng_ref[...] + lnb_ref[...]
        x1_ref[rs, :] = x1
        h2 = _layer_norm_rows(x1) * (1.0 + mod_ref[4:5, :]) + mod_ref[3:4, :]
        h2b = h2.astype(BF16)
        h2_ref[rs, :] = h2b
        info, cnt = _route(_dot_split(h2, h2b, wr_ref[...]) + br_ref[...])
        rinfo_ref[rs, :] = info
        counts = counts + cnt
    cnt_ref[0] = jnp.broadcast_to(counts, (8, LANES))


def _mixout(y, z, o, gs, ga, x2, mod_l, norm_w, w_ssd_o, w_att_o, w_o, ln_g, ln_b, w_r, b_r,
            seq, tm):
    t, d = x2.shape
    tiles_per_batch = seq // tm
    row = lambda n: pl.BlockSpec((tm, n), lambda i: (i, 0))
    return pl.pallas_call(
        _mixout_kernel,
        grid=(t // tm,),
        in_specs=[
            row(SSD_INNER), row(SSD_INNER), row(ATT_INNER), row(d), row(d), row(d),
            pl.BlockSpec((None, N_MOD, d), lambda i: (i // tiles_per_batch, 0, 0)),
            _const_spec(norm_w.shape), _const_spec(w_ssd_o.shape), _const_spec(w_att_o.shape),
            _const_spec(w_o.shape), _const_spec(ln_g.shape), _const_spec(ln_b.shape),
            _const_spec(w_r.shape), _const_spec(b_r.shape),
        ],
        out_specs=[row(d), row(d), row(LANES), pl.BlockSpec((1, 8, LANES), lambda i: (i, 0, 0))],
        out_shape=[
            jax.ShapeDtypeStruct((t, d), F32),
            jax.ShapeDtypeStruct((t, d), BF16),
            jax.ShapeDtypeStruct((t, LANES), F32),
            jax.ShapeDtypeStruct((t // tm, 8, LANES), F32),
        ],
        compiler_params=pltpu.CompilerParams(
            dimension_semantics=("arbitrary",), vmem_limit_bytes=VMEM_LIMIT),
        name="mixer_out",
    )(y, z, o, gs, ga, x2, mod_l, norm_w, w_ssd_o, w_att_o, w_o, ln_g, ln_b, w_r, b_r)


MOE_TILE = 512
GRAN = 16
MAX_GRAN = 2 * MOE_TILE // GRAN + N_EXPERTS - 1
SORT_ROWS = (MAX_GRAN + 1) * GRAN
EXP_TM = 512


def _moe_rows(t):
    rows = 2 * t + (t // MOE_TILE) * N_EXPERTS * (GRAN - 1) + N_EXPERTS * EXP_TM
    return -(-rows // EXP_TM) * EXP_TM


def _moe_plan(cnt, t):
    cnt = cnt.astype(jnp.int32)
    ng = (cnt + GRAN - 1) // GRAN
    gl = jnp.cumsum(ng, axis=1) - ng
    ngtot = jnp.sum(ng, axis=1)
    tot_g = jnp.sum(ng, axis=0)
    tm_g = EXP_TM // GRAN
    region_g = -(-tot_g // tm_g) * tm_g
    goff_g = jnp.cumsum(region_g) - region_g
    run_g = goff_g[None, :] + jnp.cumsum(ng, axis=0) - ng
    g_idx = jnp.arange(MAX_GRAN + 1, dtype=jnp.int32)
    owner = jnp.sum(g_idx[None, :, None] >= (gl + ng)[:, None, :], axis=2)
    owner = jnp.minimum(owner, N_EXPERTS - 1).astype(jnp.int32)
    is_owner = owner[:, :, None] == jnp.arange(N_EXPERTS, dtype=jnp.int32)[None, None, :]
    dst_g = jnp.sum(jnp.where(is_owner, (run_g - gl)[:, None, :], 0), axis=2) + g_idx[None, :]
    gdst = (dst_g * GRAN).astype(jnp.int32).reshape(-1)
    n_et = _moe_rows(t) // EXP_TM
    tstart_g = jnp.arange(n_et, dtype=jnp.int32) * tm_g
    te = jnp.sum(tstart_g[:, None] >= (goff_g + region_g)[None, :], axis=1)
    tec = jnp.minimum(te, N_EXPERTS - 1).astype(jnp.int32)
    valid = jnp.clip((goff_g[tec] + tot_g[tec] - tstart_g) * GRAN, 0, EXP_TM)
    valid = jnp.where(te >= N_EXPERTS, 0, valid).astype(jnp.int32)
    locoff = jnp.broadcast_to((gl * GRAN).astype(F32)[:, :, None],
                              (cnt.shape[0], N_EXPERTS, MOE_TILE))
    return ngtot.astype(jnp.int32), gdst, tec, valid, locoff


def _sorted_slots(rinfo_ref, locoff_ref, upper_ref):
    rt = jnp.transpose(rinfo_ref[...])
    i1, i2, w1, w2 = rt[0:1, :], rt[1:2, :], rt[2:3, :], rt[3:4, :]
    e_iota = lax.broadcasted_iota(jnp.int32, (N_EXPERTS, MOE_TILE), 0).astype(F32)
    a1 = e_iota == i1
    a2 = e_iota == i2
    a_t = jnp.where(a1, 1.0, jnp.where(a2, 1.0, 0.0)).astype(BF16)
    rank = jnp.dot(a_t, upper_ref[...], preferred_element_type=F32)
    pos = locoff_ref[0] + rank
    d1 = jnp.sum(jnp.where(a1, pos, 0.0), axis=0, keepdims=True)
    d2 = jnp.sum(jnp.where(a2, pos, 0.0), axis=0, keepdims=True)
    r_iota = lax.broadcasted_iota(jnp.int32, (SORT_ROWS, MOE_TILE), 0).astype(F32)
    return r_iota, d1, d2, w1, w2


def _granule_copy(src, dst, sem):
    return pltpu.make_async_copy(src, dst, sem)


def _dispatch_kernel(ngtot_s, gdst_s, h_ref, rinfo_ref, locoff_ref, upper_ref, xs_init, xs_hbm,
                     buf, sem):
    del xs_init
    i = pl.program_id(0)
    last = pl.num_programs(0) - 1
    slot = lax.rem(i, 2)

    def copy(tile, g, sl):
        src = buf.at[sl, pl.ds(pl.multiple_of(g * GRAN, GRAN), GRAN), :]
        row = pl.multiple_of(gdst_s[tile * (MAX_GRAN + 1) + g], GRAN)
        return _granule_copy(src, xs_hbm.at[pl.ds(row, GRAN), :], sem.at[sl])

    def wait_tile(tile, sl):
        def body(g, c):
            copy(tile, g, sl).wait()
            return c
        lax.fori_loop(0, ngtot_s[tile], body, 0)

    @pl.when(i >= 2)
    def _():
        wait_tile(i - 2, slot)

    r_iota, d1, d2, _, _ = _sorted_slots(rinfo_ref, locoff_ref, upper_ref)
    sel = jnp.where(r_iota == d1, 1.0, jnp.where(r_iota == d2, 1.0, 0.0)).astype(BF16)
    buf[slot] = jnp.dot(sel, h_ref[...], preferred_element_type=F32).astype(BF16)

    def start(g, c):
        copy(i, g, slot).start()
        return c
    lax.fori_loop(0, ngtot_s[i], start, 0)

    @pl.when(i == last)
    def _():
        @pl.when(i >= 1)
        def _():
            wait_tile(i - 1, 1 - slot)
        wait_tile(i, slot)


def _dispatch(plan, h2, rinfo, upper):
    ngtot, gdst, _, _, locoff = plan
    t, d = h2.shape
    n_tiles = t // MOE_TILE
    grid_spec = pltpu.PrefetchScalarGridSpec(
        num_scalar_prefetch=2,
        grid=(n_tiles,),
        in_specs=[
            pl.BlockSpec((MOE_TILE, d), lambda i, *_: (i, 0)),
            pl.BlockSpec((MOE_TILE, LANES), lambda i, *_: (i, 0)),
            pl.BlockSpec((1, N_EXPERTS, MOE_TILE), lambda i, *_: (i, 0, 0)),
            pl.BlockSpec((MOE_TILE, MOE_TILE), lambda i, *_: (0, 0)),
            pl.BlockSpec(memory_space=pl.ANY),
        ],
        out_specs=pl.BlockSpec(memory_space=pl.ANY),
        scratch_shapes=[pltpu.VMEM((2, SORT_ROWS, d), BF16), pltpu.SemaphoreType.DMA((2,))],
    )
    xs_init = jnp.zeros((_moe_rows(t), d), BF16)
    return pl.pallas_call(
        _dispatch_kernel,
        grid_spec=grid_spec,
        out_shape=jax.ShapeDtypeStruct((_moe_rows(t), d), BF16),
        input_output_aliases={6: 0},
        compiler_params=pltpu.CompilerParams(
            dimension_semantics=("arbitrary",), vmem_limit_bytes=VMEM_LIMIT),
        name="moe_dispatch",
    )(ngtot, gdst, h2, rinfo, locoff, upper, xs_init)


def _expert_kernel(te_s, valid_s, x_ref, wg_ref, wu_ref, wd_ref, y_ref):
    valid = valid_s[pl.program_id(0)]

    @pl.when(valid > 0)
    def _():
        x = x_ref[...]
        a = jnp.dot(x, wg_ref[0], preferred_element_type=F32)
        u = jnp.dot(x, wu_ref[0], preferred_element_type=F32)
        hid = ((a * _sigmoid(a)) * u).astype(BF16)
        y_ref[...] = jnp.dot(hid, wd_ref[0], preferred_element_type=F32).astype(y_ref.dtype)

    @pl.when(valid <= 0)
    def _():
        y_ref[...] = jnp.zeros_like(y_ref)


def _experts(plan, xs, w_gate, w_up, w_down):
    _, _, tec, valid, _ = plan
    rows, d = xs.shape
    _, _, ff = w_gate.shape
    grid_spec = pltpu.PrefetchScalarGridSpec(
        num_scalar_prefetch=2,
        grid=(rows // EXP_TM,),
        in_specs=[
            pl.BlockSpec((EXP_TM, d), lambda i, te, va: (i, 0)),
            pl.BlockSpec((1, d, ff), lambda i, te, va: (te[i], 0, 0)),
            pl.BlockSpec((1, d, ff), lambda i, te, va: (te[i], 0, 0)),
            pl.BlockSpec((1, ff, d), lambda i, te, va: (te[i], 0, 0)),
        ],
        out_specs=pl.BlockSpec((EXP_TM, d), lambda i, te, va: (i, 0)),
    )
    return pl.pallas_call(
        _expert_kernel,
        grid_spec=grid_spec,
        out_shape=jax.ShapeDtypeStruct((rows, d), BF16),
        compiler_params=pltpu.CompilerParams(
            dimension_semantics=("arbitrary",), vmem_limit_bytes=VMEM_LIMIT),
        name="moe_experts",
    )(tec, valid, xs, w_gate, w_up, w_down)


def _combine_kernel(ngtot_s, gdst_s, ys_hbm, rinfo_ref, locoff_ref, upper_ref, x1_ref, mod_ref,
                    lng_ref, lnb_ref, o_ref, buf, sem):
    i = pl.program_id(0)
    last = pl.num_programs(0) - 1
    slot = lax.rem(i, 2)

    def copy(tile, g, sl):
        row = pl.multiple_of(gdst_s[tile * (MAX_GRAN + 1) + g], GRAN)
        dst = buf.at[sl, pl.ds(pl.multiple_of(g * GRAN, GRAN), GRAN), :]
        return _granule_copy(ys_hbm.at[pl.ds(row, GRAN), :], dst, sem.at[sl])

    def start_tile(tile, sl):
        def body(g, c):
            copy(tile, g, sl).start()
            return c
        lax.fori_loop(0, ngtot_s[tile], body, 0)

    @pl.when(i == 0)
    def _():
        buf[...] = jnp.zeros_like(buf)
        start_tile(0, 0)

    @pl.when(i < last)
    def _():
        start_tile(i + 1, 1 - slot)

    def wait(g, c):
        copy(i, g, slot).wait()
        return c
    lax.fori_loop(0, ngtot_s[i], wait, 0)

    r_iota, d1, d2, w1, w2 = _sorted_slots(rinfo_ref, locoff_ref, upper_ref)
    wsel = jnp.where(r_iota == d1, w1, jnp.where(r_iota == d2, w2, 0.0)).astype(BF16)
    moe = lax.dot_general(wsel, buf[slot], (((0,), (0,)), ((), ())),
                          preferred_element_type=F32)
    gate2 = mod_ref[5:6, :]
    xn = _layer_norm_rows(DEEPNORM_ALPHA * x1_ref[...] + (1.0 + gate2) * moe)
    o_ref[...] = xn * lng_ref[...] + lnb_ref[...]


def _combine(plan, ys, rinfo, upper, x1, mod_l, ln_g, ln_b, seq):
    ngtot, gdst, _, _, locoff = plan
    t, d = x1.shape
    tiles_per_batch = seq // MOE_TILE
    grid_spec = pltpu.PrefetchScalarGridSpec(
        num_scalar_prefetch=2,
        grid=(t // MOE_TILE,),
        in_specs=[
            pl.BlockSpec(memory_space=pl.ANY),
            pl.BlockSpec((MOE_TILE, LANES), lambda i, *_: (i, 0)),
            pl.BlockSpec((1, N_EXPERTS, MOE_TILE), lambda i, *_: (i, 0, 0)),
            pl.BlockSpec((MOE_TILE, MOE_TILE), lambda i, *_: (0, 0)),
            pl.BlockSpec((MOE_TILE, d), lambda i, *_: (i, 0)),
            pl.BlockSpec((None, N_MOD, d), lambda i, *_: (i // tiles_per_batch, 0, 0)),
            pl.BlockSpec((1, d), lambda i, *_: (0, 0)),
            pl.BlockSpec((1, d), lambda i, *_: (0, 0)),
        ],
        out_specs=pl.BlockSpec((MOE_TILE, d), lambda i, *_: (i, 0)),
        scratch_shapes=[pltpu.VMEM((2, SORT_ROWS, d), BF16), pltpu.SemaphoreType.DMA((2,))],
    )
    return pl.pallas_call(
        _combine_kernel,
        grid_spec=grid_spec,
        out_shape=jax.ShapeDtypeStruct((t, d), F32),
        compiler_params=pltpu.CompilerParams(
            dimension_semantics=("arbitrary",), vmem_limit_bytes=VMEM_LIMIT),
        name="moe_combine",
    )(ngtot, gdst, ys, rinfo, locoff, upper, x1, mod_l, ln_g, ln_b)


def _pad_lanes(a):
    return jnp.pad(a, ((0, 0), (0, LANES - a.shape[1])))


def kernel(x, c, w_mod, b_mod, w_in, conv_w, conv_b, dt_bias, a_log, d_skip, ssd_norm_w, forget_b,
           w_ssd_o, w_att_o, w_o, ln1_g, ln1_b, w_router_group, b_router_group, w_router_expert,
           b_router_expert, w_gate, w_up, w_down, ln2_g, ln2_b):
    bsz, seq, d = x.shape
    depth = w_mod.shape[0]
    t = bsz * seq
    assert seq % MOE_TILE == 0 and d == SSD_INNER
    tm_proj = min(512, seq)
    ssd_rows = min(512, seq)
    upper = jnp.asarray(np.triu(np.ones((MOE_TILE, MOE_TILE), np.float32), k=1), BF16)
    tq = min(512, seq)
    tk = min(512, seq)

    mod = _modulation(c, w_mod, b_mod).reshape(depth, bsz, N_MOD, d)
    consts = _ssd_constants()
    placement = _attn_placement()
    in_sizes = (SSD_INNER, SSD_CONV_CH, SSD_HEADS, ATT_INNER, ATT_INNER, ATT_INNER, ATT_HEADS, d, d)
    offs = np.concatenate([[0], np.cumsum(in_sizes)]).tolist()

    x2 = x.reshape(t, d)
    for l in range(depth):
        cols = [w_in[l][:, offs[i]:offs[i + 1]] for i in range(len(in_sizes))]
        wz, wxbc, wdt, wq, wk, wv, wf, wgs, wga = cols
        wq = wq * (math.log2(math.e) * ATT_HEAD_DIM ** -0.5)
        big = [w.astype(BF16) for w in (wz, wxbc, wq, wk, wv, wgs, wga)]
        w_small = _pad_lanes(jnp.concatenate([wdt, wf], axis=1))
        z, xbc, q, k, v, gs, ga, small = _inproj(x2, mod[l], big, w_small, seq, tm_proj)

        hvec = jnp.concatenate([
            _pad_lanes(jnp.concatenate([dt_bias[l], forget_b[l]])[None, :]),
            _pad_lanes(a_log[l][None, :]),
            jnp.zeros((6, LANES), F32)], axis=0)
        dskip_x = jnp.repeat(d_skip[l], SSD_HEAD_DIM)[None, :]
        y, fcum = _ssd(xbc.reshape(bsz, seq, SSD_CONV_CH), small.reshape(bsz, seq, LANES),
                       conv_w[l], conv_b[l][None, :], hvec, dskip_x, consts, ssd_rows)

        o = _attention(q.reshape(bsz, seq, ATT_INNER), k.reshape(bsz, seq, ATT_INNER),
                       v.reshape(bsz, seq, ATT_INNER), fcum, placement, tq, tk)

        w_r = _pad_lanes(jnp.concatenate([w_router_expert[l], w_router_group[l]], axis=1))
        b_r = _pad_lanes(jnp.concatenate([b_router_expert[l], b_router_group[l]])[None, :])
        x1, h2, rinfo, cnt = _mixout(
            y.reshape(t, SSD_INNER), z, o.reshape(t, ATT_INNER), gs, ga, x2, mod[l],
            ssd_norm_w[l][None, :], w_ssd_o[l].astype(BF16), w_att_o[l].astype(BF16),
            w_o[l].astype(BF16), ln1_g[l][None, :], ln1_b[l][None, :], w_r, b_r, seq, MOE_TILE)

        plan = _moe_plan(cnt[:, 0, :N_EXPERTS], t)
        xs = _dispatch(plan, h2, rinfo, upper)
        ys = _experts(plan, xs, w_gate[l].astype(BF16), w_up[l].astype(BF16),
                      w_down[l].astype(BF16))
        x2 = _combine(plan, ys, rinfo, upper, x1, mod[l], ln2_g[l][None, :], ln2_b[l][None, :],
                      seq)
    return x2.reshape(bsz, seq, d)
```

```python
import functools
import math

import numpy as np
import jax
import jax.numpy as jnp
from jax import lax
from jax.experimental import pallas as pl
from jax.experimental.pallas import tpu as pltpu

SSD_HEAD_DIM = 64
SSD_HEADS = 16
SSD_GROUPS = 4
SSD_STATE = 128
SSD_CHUNK = 128
SSD_CONV = 4
SSD_INNER = SSD_HEADS * SSD_HEAD_DIM
SSD_BC = SSD_GROUPS * SSD_STATE
SSD_CONV_CH = SSD_INNER + 2 * SSD_BC
ATT_HEAD_DIM = 64
ATT_HEADS = 8
ATT_INNER = ATT_HEADS * ATT_HEAD_DIM
N_GROUPS = 4
EXPERTS_PER_GROUP = 4
N_EXPERTS = 16
N_MOD = 6
DEPTH_FOR_DEEPNORM = 4
DEEPNORM_ALPHA = (2 * DEPTH_FOR_DEEPNORM) ** 0.25
LN_EPS = 1e-5

LANES = 128
VMEM_LIMIT = 56 * 1024 * 1024

F32 = jnp.float32
BF16 = jnp.bfloat16
HIGHEST = lax.Precision.HIGHEST


def _sigmoid(x):
    return 1.0 / (1.0 + jnp.exp2(x * (-math.log2(math.e))))


def _layer_norm_rows(x):
    mu = jnp.mean(x, axis=-1, keepdims=True)
    xc = x - mu
    var = jnp.mean(xc * xc, axis=-1, keepdims=True)
    return xc * lax.rsqrt(var + LN_EPS)


def _dot_split(h, h_hi, w):
    h_lo = (h - h_hi.astype(F32)).astype(BF16)
    w_hi = w.astype(BF16)
    w_lo = (w - w_hi.astype(F32)).astype(BF16)
    r = jnp.dot(h_hi, jnp.concatenate([w_hi, w_lo], axis=1), preferred_element_type=F32)
    return r[:, :LANES] + r[:, LANES:] + jnp.dot(h_lo, w_hi, preferred_element_type=F32)


def _const_spec(shape):
    n = len(shape)
    return pl.BlockSpec(shape, lambda *_: (0,) * n)


def _mod_kernel(c_ref, w_ref, b_ref, o_ref):
    c = c_ref[...]
    ca = c * _sigmoid(c)
    o_ref[0] = jnp.dot(ca, w_ref[0], precision=HIGHEST, preferred_element_type=F32) + b_ref[0]


def _modulation(c, w_mod, b_mod):
    depth, d, n = w_mod.shape
    bsz = c.shape[0]
    tn = 1024
    return pl.pallas_call(
        _mod_kernel,
        grid=(depth, n // tn),
        in_specs=[
            pl.BlockSpec((bsz, d), lambda l, j: (0, 0)),
            pl.BlockSpec((1, d, tn), lambda l, j: (l, 0, j)),
            pl.BlockSpec((1, 1, tn), lambda l, j: (l, 0, j)),
        ],
        out_specs=pl.BlockSpec((1, bsz, tn), lambda l, j: (l, 0, j)),
        out_shape=jax.ShapeDtypeStruct((depth, bsz, n), F32),
        compiler_params=pltpu.CompilerParams(
            dimension_semantics=("arbitrary", "arbitrary"), vmem_limit_bytes=VMEM_LIMIT),
        name="modulation",
    )(c, w_mod, b_mod.reshape(depth, 1, n))


def _inproj_kernel(x_ref, mod_ref, wz_ref, wxbc_ref, wq_ref, wk_ref, wv_ref, wgs_ref, wga_ref,
                   wsm_ref, z_ref, xbc_ref, q_ref, k_ref, v_ref, gs_ref, ga_ref, sm_ref):
    x = x_ref[...]
    shift = mod_ref[0:1, :]
    scale = mod_ref[1:2, :]
    h = _layer_norm_rows(x) * (1.0 + scale) + shift
    hb = h.astype(BF16)
    for w_ref, o_ref in ((wz_ref, z_ref), (wxbc_ref, xbc_ref), (wq_ref, q_ref), (wk_ref, k_ref),
                         (wv_ref, v_ref), (wgs_ref, gs_ref), (wga_ref, ga_ref)):
        o_ref[...] = jnp.dot(hb, w_ref[...], preferred_element_type=F32).astype(o_ref.dtype)
    sm_ref[...] = _dot_split(h, hb, wsm_ref[...])


def _inproj(x2, mod_l, weights, w_small, seq, tm):
    t, d = x2.shape
    tiles_per_batch = seq // tm
    widths = [w.shape[1] for w in weights]
    in_specs = [
        pl.BlockSpec((tm, d), lambda i: (i, 0)),
        pl.BlockSpec((None, N_MOD, d), lambda i: (i // tiles_per_batch, 0, 0)),
    ]
    in_specs += [_const_spec(w.shape) for w in weights]
    in_specs += [_const_spec(w_small.shape)]
    out_specs = [pl.BlockSpec((tm, n), lambda i: (i, 0)) for n in widths]
    out_specs += [pl.BlockSpec((tm, LANES), lambda i: (i, 0))]
    out_shape = [jax.ShapeDtypeStruct((t, n), BF16) for n in widths]
    out_shape += [jax.ShapeDtypeStruct((t, LANES), F32)]
    return pl.pallas_call(
        _inproj_kernel,
        grid=(t // tm,),
        in_specs=in_specs,
        out_specs=out_specs,
        out_shape=out_shape,
        compiler_params=pltpu.CompilerParams(
            dimension_semantics=("arbitrary",), vmem_limit_bytes=VMEM_LIMIT),
        name="inproj",
    )(x2, mod_l, *weights, w_small)


CONV_HIST = 16
CONV_WIN = CONV_HIST + SSD_CHUNK


def _ssd_constants():
    L = SSD_CHUNK
    sh = np.zeros((L, SSD_CONV * CONV_WIN), np.float32)
    for j in range(SSD_CONV):
        for t in range(L):
            sh[t, j * CONV_WIN + CONV_HIST + t - j] = 1.0
    tril = np.tril(np.ones((L, L), np.float32))
    e3 = np.zeros((LANES, SSD_INNER), np.float32)
    for piece in range(3):
        for h in range(SSD_HEADS):
            e3[32 * piece + h, h * SSD_HEAD_DIM:(h + 1) * SSD_HEAD_DIM] = 1.0
    r = SSD_HEADS // SSD_GROUPS
    bd = np.zeros((r * L, r * SSD_HEAD_DIM), np.float32)
    for i in range(r):
        bd[i * L:(i + 1) * L, i * SSD_HEAD_DIM:(i + 1) * SSD_HEAD_DIM] = 1.0
    return (jnp.asarray(sh, BF16), jnp.asarray(tril, BF16), jnp.asarray(e3, BF16),
            jnp.asarray(bd, BF16))


def _split3_bf16(a):
    hi = a.astype(BF16)
    r1 = a - hi.astype(F32)
    mid = r1.astype(BF16)
    lo = (r1 - mid.astype(F32)).astype(BF16)
    return hi, mid, lo


def _expand_heads(xm, e3):
    hi, mid, lo = _split3_bf16(xm)
    packed = (hi.astype(F32) + pltpu.roll(mid.astype(F32), 32, 1)
              + pltpu.roll(lo.astype(F32), 64, 1)).astype(BF16)
    return jnp.dot(packed, e3, preferred_element_type=F32)


def _ssd_kernel(xbc_ref, sm_ref, cw_ref, cb_ref, hv_ref, dskip_ref, sh_ref, tril_ref, e3_ref,
                bd_ref, y_ref, fcum_ref, uext, state, fcarry, *, rows):
    L = SSD_CHUNK
    j = pl.program_id(1)

    @pl.when(j == 0)
    def _():
        uext[0:CONV_HIST, :] = jnp.zeros((CONV_HIST, SSD_CONV_CH), BF16)
        state[...] = jnp.zeros_like(state)
        fcarry[...] = jnp.zeros_like(fcarry)

    uext[CONV_HIST:CONV_HIST + rows, :] = xbc_ref[0]
    cw_b = cw_ref[...].astype(BF16)

    lane = lax.broadcasted_iota(jnp.int32, (1, LANES), 1)
    is_dt = lane < SSD_HEADS
    is_f = jnp.logical_and(lane >= SSD_HEADS, lane < SSD_HEADS + ATT_HEADS)
    a_neg = -jnp.exp(hv_ref[1:2, :])
    bias = hv_ref[0:1, :]
    rr = lax.broadcasted_iota(jnp.int32, (L, L), 0)
    cc = lax.broadcasted_iota(jnp.int32, (L, L), 1)
    causal = rr >= cc
    gw = (SSD_HEADS // SSD_GROUPS) * SSD_HEAD_DIM

    for c in range(rows // L):
        r0 = c * L
        win = uext[r0:r0 + CONV_WIN, :]
        taps = jnp.concatenate([win * cw_b[SSD_CONV - 1 - j:SSD_CONV - j, :]
                                for j in range(SSD_CONV)], axis=0)
        conv = jnp.dot(sh_ref[...], taps, preferred_element_type=F32) + cb_ref[...]
        act = conv * _sigmoid(conv)
        xs = act[:, :SSD_INNER]
        bm = act[:, SSD_INNER:SSD_INNER + SSD_BC]
        cm = act[:, SSD_INNER + SSD_BC:]

        pre = sm_ref[0, r0:r0 + L, :] + bias
        tail = jnp.log(1.0 + jnp.exp(-jnp.abs(pre)))
        dt = jnp.where(is_dt, jnp.maximum(pre, 0.0) + tail, 0.0)
        log_f = jnp.minimum(pre, 0.0) - tail
        comb = jnp.where(is_dt, dt * a_neg, jnp.where(is_f, log_f, 0.0))
        c3 = jnp.dot(tril_ref[...], jnp.concatenate(_split3_bf16(comb), axis=1),
                     preferred_element_type=F32)
        cs = c3[:, :LANES] + c3[:, LANES:2 * LANES] + c3[:, 2 * LANES:]
        fc = cs + fcarry[...]
        fcarry[...] = jnp.where(is_f, fc[L - 1:L, :], 0.0)
        f_hi, f_mid, f_lo = _split3_bf16(jnp.where(is_f, fc * math.log2(math.e), 0.0))
        fcum_ref[0, r0:r0 + L, :] = (
            f_hi.astype(F32) + pltpu.roll(f_mid.astype(F32), 32, 1)
            + pltpu.roll(f_lo.astype(F32), 64, 1)
            + jnp.where(lane == LANES - 1, 1.0, 0.0)).astype(BF16)

        a_cs = jnp.where(is_dt, cs, 0.0)
        a_cs_x = _expand_heads(a_cs, e3_ref[...])
        dt_x = _expand_heads(dt, e3_ref[...])
        a_last_x = a_cs_x[L - 1:L, :]
        xdt = xs * dt_x
        xdt_b = xdt.astype(BF16)
        xend_b = (xdt * jnp.exp(a_last_x - a_cs_x)).astype(BF16)
        e_acs = jnp.exp(a_cs_x)
        e_last = jnp.exp(a_last_x)
        a_cs_t = jnp.transpose(a_cs)

        y_groups = []
        for g in range(SSD_GROUPS):
            bg = bm[:, g * SSD_STATE:(g + 1) * SSD_STATE]
            cg_b = cm[:, g * SSD_STATE:(g + 1) * SSD_STATE].astype(BF16)
            cb = lax.dot_general(cg_b, bg.astype(BF16), (((1,), (1,)), ((), ())),
                                 preferred_element_type=F32)
            ms = []
            for r in range(SSD_HEADS // SSD_GROUPS):
                h = g * (SSD_HEADS // SSD_GROUPS) + r
                seg = a_cs[:, h:h + 1] - a_cs_t[h:h + 1, :]
                dec = jnp.exp(jnp.where(causal, seg, -jnp.inf))
                ms.append((cb * dec).astype(BF16))
            m_cat = jnp.concatenate(ms, axis=1)
            xg = xdt_b[:, g * gw:(g + 1) * gw]
            x_bd = jnp.concatenate([xg] * (SSD_HEADS // SSD_GROUPS), axis=0) * bd_ref[...]
            y_diag = jnp.dot(m_cat, x_bd, preferred_element_type=F32)
            st = state[:, g * gw:(g + 1) * gw]
            y_off = jnp.dot(cg_b, st.astype(BF16), preferred_element_type=F32)
            y_groups.append(y_diag + y_off * e_acs[:, g * gw:(g + 1) * gw])
            new_st = jnp.dot(jnp.transpose(bg).astype(BF16), xend_b[:, g * gw:(g + 1) * gw],
                             preferred_element_type=F32)
            state[:, g * gw:(g + 1) * gw] = st * e_last[:, g * gw:(g + 1) * gw] + new_st
        y = jnp.concatenate(y_groups, axis=1) + dskip_ref[...] * xs
        y_ref[0, r0:r0 + L, :] = y.astype(y_ref.dtype)

    uext[0:CONV_HIST, :] = uext[rows:rows + CONV_HIST, :]


def _ssd(xbc, small, conv_w, conv_b, hvec, dskip_x, consts, rows):
    bsz, seq, _ = xbc.shape
    sh, tril, e3, bd = consts
    kern = functools.partial(_ssd_kernel, rows=rows)
    return pl.pallas_call(
        kern,
        grid=(bsz, seq // rows),
        in_specs=[
            pl.BlockSpec((1, rows, SSD_CONV_CH), lambda b, j: (b, j, 0)),
            pl.BlockSpec((1, rows, LANES), lambda b, j: (b, j, 0)),
            _const_spec(conv_w.shape), _const_spec(conv_b.shape), _const_spec(hvec.shape),
            _const_spec(dskip_x.shape), _const_spec(sh.shape), _const_spec(tril.shape),
            _const_spec(e3.shape), _const_spec(bd.shape),
        ],
        out_specs=[
            pl.BlockSpec((1, rows, SSD_INNER), lambda b, j: (b, j, 0)),
            pl.BlockSpec((1, rows, LANES), lambda b, j: (b, j, 0)),
        ],
        out_shape=[
            jax.ShapeDtypeStruct((bsz, seq, SSD_INNER), BF16),
            jax.ShapeDtypeStruct((bsz, seq, LANES), BF16),
        ],
        scratch_shapes=[
            pltpu.VMEM((CONV_HIST + rows, SSD_CONV_CH), BF16),
            pltpu.VMEM((SSD_STATE, SSD_INNER), F32),
            pltpu.VMEM((1, LANES), F32),
        ],
        compiler_params=pltpu.CompilerParams(
            dimension_semantics=("arbitrary", "arbitrary"), vmem_limit_bytes=VMEM_LIMIT),
        name="ssd",
    )(xbc, small, conv_w, conv_b, hvec, dskip_x, sh, tril, e3, bd)


def _attn_placement():
    pq = np.zeros((ATT_HEADS // 2, LANES, 2 * LANES), np.float32)
    pk = np.zeros_like(pq)
    for p in range(ATT_HEADS // 2):
        for hd in range(2):
            h = 2 * p + hd
            col = hd * LANES + (ATT_HEAD_DIM if hd == 0 else 0)
            for piece in range(3):
                pq[p, 32 * piece + SSD_HEADS + h, col + piece] = 1.0
                pq[p, LANES - 1, col + 3 + piece] = 1.0
                pk[p, LANES - 1, col + piece] = 1.0
                pk[p, 32 * piece + SSD_HEADS + h, col + 3 + piece] = -1.0
    return jnp.asarray(pq, BF16), jnp.asarray(pk, BF16)


def _attn_kernel(q_ref, k_ref, v_ref, fq_ref, fk_ref, pq_ref, pk_ref, o_ref, kaug_sc, s_sc,
                 m_sc, acc_sc, *, tq, tk):
    qi = pl.program_id(2)
    seq = k_ref.shape[1]
    r = tq // tk
    n_full = qi * r
    key_idx = lax.broadcasted_iota(jnp.int32, (tk, tq), 0)
    qry_idx = lax.broadcasted_iota(jnp.int32, (tk, tq), 1)
    lane2 = lax.broadcasted_iota(jnp.int32, (1, 2 * LANES), 1)
    keep = jnp.logical_or(lane2 < ATT_HEAD_DIM, lane2 >= LANES + ATT_HEAD_DIM)

    def augment(x, f, place):
        aug = jnp.dot(f, place, preferred_element_type=F32).astype(BF16)
        return jnp.where(keep, jnp.concatenate([x, x], axis=1), aug)

    npair = q_ref.shape[2] // LANES
    heads = range(2 * npair)
    pair_cols = lambda pp: slice(pp * LANES, (pp + 1) * LANES)
    slab_cols = lambda h: slice(h * LANES, (h + 1) * LANES)

    @pl.when(qi == 0)
    def _():
        def fill(c, carry):
            rows = pl.ds(pl.multiple_of(c * tk, tk), tk)
            for pp in range(npair):
                kaug_sc[rows, pp * 2 * LANES:(pp + 1) * 2 * LANES] = augment(
                    k_ref[0, rows, pair_cols(pp)], fk_ref[0, rows, :], pk_ref[pp])
            return carry
        lax.fori_loop(0, seq // tk, fill, 0)

    q_aug_t = []
    for pp in range(npair):
        qa = augment(q_ref[0, :, pair_cols(pp)], fq_ref[0], pq_ref[pp]).astype(F32)
        q_aug_t += [jnp.transpose(qa[:, slab_cols(hd)]).astype(BF16) for hd in range(2)]

    def scores(kblk):
        start = pl.multiple_of(kblk * tk, tk)
        kb = kaug_sc[pl.ds(start, tk), :]
        return [jnp.dot(kb[:, slab_cols(h)], q_aug_t[h], preferred_element_type=F32)
                for h in heads]

    def prepare(s_list, diag):
        mcols = []
        for h in heads:
            s_t = s_list[h]
            if diag is not None:
                s_t = jnp.where(key_idx + diag * tk <= qry_idx, s_t, -jnp.inf)
            s_sc[h] = s_t
            mcols.append(jnp.max(s_t, axis=0, keepdims=True))
        return mcols

    def consume(kblk, mcols):
        start = pl.multiple_of(kblk * tk, tk)
        vb = v_ref[0, pl.ds(start, tk), :]
        for h in heads:
            m_old = m_sc[h]
            m_new = jnp.maximum(m_old, mcols[h])
            p_t = jnp.exp2((s_sc[h] - m_new).astype(BF16))
            alpha = jnp.exp2(m_old - m_new)
            vb_p = vb[:, pair_cols(h // 2)]
            vb_h = jnp.where(own_lanes[h % 2], vb_p, jnp.ones_like(vb_p))
            pv = lax.dot_general(vb_h, p_t, (((0,), (0,)), ((), ())),
                                 preferred_element_type=F32)
            acc_sc[h] = alpha * acc_sc[h] + pv
            m_sc[h] = m_new

    lane1 = lax.broadcasted_iota(jnp.int32, (1, LANES), 1)
    own_lanes = (lane1 < ATT_HEAD_DIM, lane1 >= ATT_HEAD_DIM)
    m_sc[...] = jnp.full_like(m_sc, -jnp.inf)
    acc_sc[...] = jnp.zeros_like(acc_sc)
    mcols = prepare(scores(n_full), 0)
    for d in range(r):
        s_next = scores(n_full + d + 1 if d + 1 < r else 0)
        consume(n_full + d, mcols)
        mcols = prepare(s_next, d + 1 if d + 1 < r else None)

    def body(jb, mcols):
        s_next = scores(jnp.minimum(jb + 1, n_full - 1))
        consume(jb, mcols)
        return tuple(prepare(s_next, None))

    lax.fori_loop(0, n_full, body, tuple(mcols))
    hdim = ATT_HEAD_DIM
    for pp in range(npair):
        acc0, acc1 = acc_sc[2 * pp], acc_sc[2 * pp + 1]
        o_t = jnp.concatenate([acc0[:hdim] / acc0[hdim:hdim + 1], acc1[hdim:] / acc1[0:1]],
                              axis=0)
        o_ref[0, :, pair_cols(pp)] = jnp.transpose(o_t).astype(o_ref.dtype)


ATT_PAIRS_PER_STEP = 2


def _attention(q, k, v, f_pieces, placement, tq, tk):
    bsz, seq, _ = q.shape
    npair = ATT_PAIRS_PER_STEP
    steps = ATT_HEADS // (2 * npair)
    pq, pk = placement
    kern = functools.partial(_attn_kernel, tq=tq, tk=tk)
    wide = npair * LANES
    return pl.pallas_call(
        kern,
        grid=(bsz, steps, seq // tq),
        in_specs=[
            pl.BlockSpec((1, tq, wide), lambda b, p, i: (b, i, p)),
            pl.BlockSpec((1, seq, wide), lambda b, p, i: (b, 0, p)),
            pl.BlockSpec((1, seq, wide), lambda b, p, i: (b, 0, p)),
            pl.BlockSpec((1, tq, LANES), lambda b, p, i: (b, i, 0)),
            pl.BlockSpec((1, seq, LANES), lambda b, p, i: (b, 0, 0)),
            pl.BlockSpec((npair, LANES, 2 * LANES), lambda b, p, i: (p, 0, 0)),
            pl.BlockSpec((npair, LANES, 2 * LANES), lambda b, p, i: (p, 0, 0)),
        ],
        out_specs=pl.BlockSpec((1, tq, wide), lambda b, p, i: (b, i, p)),
        out_shape=jax.ShapeDtypeStruct((bsz, seq, ATT_INNER), BF16),
        scratch_shapes=[pltpu.VMEM((seq, 2 * wide), BF16),
                        pltpu.VMEM((2 * npair, tk, tq), F32),
                        pltpu.VMEM((2 * npair, 1, tq), F32),
                        pltpu.VMEM((2 * npair, LANES, tq), F32)],
        compiler_params=pltpu.CompilerParams(
            dimension_semantics=("arbitrary", "arbitrary", "arbitrary"),
            vmem_limit_bytes=VMEM_LIMIT),
        name="fox_attention",
    )(q, k, v, f_pieces, f_pieces, pq, pk)


def _route(logits):
    lane = lax.broadcasted_iota(jnp.int32, logits.shape, 1)
    neg = -jnp.inf
    big = jnp.int32(1 << 20)
    is_g = jnp.logical_and(lane >= N_EXPERTS, lane < N_EXPERTS + N_GROUPS)
    gl = jnp.where(is_g, logits, neg)
    gmax = jnp.max(gl, axis=1, keepdims=True)
    gsum = jnp.sum(jnp.exp(gl - gmax), axis=1, keepdims=True)
    g_p = 1.0 / gsum
    g_idx = jnp.min(jnp.where(gl == gmax, lane, big), axis=1, keepdims=True) - N_EXPERTS
    lo = g_idx * EXPERTS_PER_GROUP
    in_group = jnp.logical_and(lane >= lo, lane < lo + EXPERTS_PER_GROUP)
    el = jnp.where(in_group, logits, neg)
    m1 = jnp.max(el, axis=1, keepdims=True)
    i1 = jnp.min(jnp.where(el == m1, lane, big), axis=1, keepdims=True)
    el2 = jnp.where(lane == i1, neg, el)
    m2 = jnp.max(el2, axis=1, keepdims=True)
    i2 = jnp.min(jnp.where(el2 == m2, lane, big), axis=1, keepdims=True)
    e2 = jnp.exp(m2 - m1)
    w1 = g_p / (1.0 + e2)
    w2 = g_p * e2 / (1.0 + e2)
    info = jnp.where(lane == 0, i1.astype(F32), jnp.where(lane == 1, i2.astype(F32),
                     jnp.where(lane == 2, w1, jnp.where(lane == 3, w2, 0.0))))
    chosen = jnp.where(lane == i1, 1.0, jnp.where(lane == i2, 1.0, 0.0))
    return info, jnp.sum(chosen, axis=0, keepdims=True)


def _mixout_kernel(y_ref, z_ref, o_ref, gs_ref, ga_ref, x_ref, mod_ref, nw_ref, wso_ref, wao_ref,
                   wo_ref, lng_ref, lnb_ref, wr_ref, br_ref, x1_ref, h2_ref, rinfo_ref, cnt_ref):
    tm = x_ref.shape[0]
    parts = 2
    rows_per = tm // parts
    counts = jnp.zeros((1, LANES), F32)
    for part in range(parts):
        rs = slice(part * rows_per, (part + 1) * rows_per)
        g = y_ref[rs, :].astype(F32)
        z = z_ref[rs, :].astype(F32)
        g = g * (z * _sigmoid(z))
        g = g * lax.rsqrt(jnp.mean(g * g, axis=-1, keepdims=True) + LN_EPS) * nw_ref[...]
        y_ssd = jnp.dot(g.astype(BF16), wso_ref[...], preferred_element_type=F32)
        y_att = jnp.dot(o_ref[rs, :], wao_ref[...], preferred_element_type=F32)
        merged = (_sigmoid(gs_ref[rs, :].astype(F32)) * y_ssd
                  + _sigmoid(ga_ref[rs, :].astype(F32)) * y_att)
        mix = jnp.dot(merged.astype(BF16), wo_ref[...], preferred_element_type=F32)
        gate1 = mod_ref[2:3, :]
        x1 = _layer_norm_rows(DEEPNORM_ALPHA * x_ref[rs, :] + (1.0 + gate1) * mix)
        x1 = x1 * lng_ref[...] + lnb_ref[...]
        x1_ref[rs, :] = x1
        h2 = _layer_norm_rows(x1) * (1.0 + mod_ref[4:5, :]) + mod_ref[3:4, :]
        h2b = h2.astype(BF16)
        h2_ref[rs, :] = h2b
        info, cnt = _route(_dot_split(h2, h2b, wr_ref[...]) + br_ref[...])
        rinfo_ref[rs, :] = info
        counts = counts + cnt
    cnt_ref[0] = jnp.broadcast_to(counts, (8, LANES))


def _mixout(y, z, o, gs, ga, x2, mod_l, norm_w, w_ssd_o, w_att_o, w_o, ln_g, ln_b, w_r, b_r,
            seq, tm):
    t, d = x2.shape
    tiles_per_batch = seq // tm
    row = lambda n: pl.BlockSpec((tm, n), lambda i: (i, 0))
    return pl.pallas_call(
        _mixout_kernel,
        grid=(t // tm,),
        in_specs=[
            row(SSD_INNER), row(SSD_INNER), row(ATT_INNER), row(d), row(d), row(d),
            pl.BlockSpec((None, N_MOD, d), lambda i: (i // tiles_per_batch, 0, 0)),
            _const_spec(norm_w.shape), _const_spec(w_ssd_o.shape), _const_spec(w_att_o.shape),
            _const_spec(w_o.shape), _const_spec(ln_g.shape), _const_spec(ln_b.shape),
            _const_spec(w_r.shape), _const_spec(b_r.shape),
        ],
        out_specs=[row(d), row(d), row(LANES), pl.BlockSpec((1, 8, LANES), lambda i: (i, 0, 0))],
        out_shape=[
            jax.ShapeDtypeStruct((t, d), F32),
            jax.ShapeDtypeStruct((t, d), BF16),
            jax.ShapeDtypeStruct((t, LANES), F32),
            jax.ShapeDtypeStruct((t // tm, 8, LANES), F32),
        ],
        compiler_params=pltpu.CompilerParams(
            dimension_semantics=("arbitrary",), vmem_limit_bytes=VMEM_LIMIT),
        name="mixer_out",
    )(y, z, o, gs, ga, x2, mod_l, norm_w, w_ssd_o, w_att_o, w_o, ln_g, ln_b, w_r, b_r)


MOE_TILE = 512
GRAN = 16
MAX_GRAN = 2 * MOE_TILE // GRAN + N_EXPERTS - 1
SORT_ROWS = (MAX_GRAN + 1) * GRAN
EXP_TM = 512


def _moe_rows(t):
    rows = 2 * t + (t // MOE_TILE) * N_EXPERTS * (GRAN - 1) + N_EXPERTS * EXP_TM
    return -(-rows // EXP_TM) * EXP_TM


def _moe_plan(cnt, t):
    cnt = cnt.astype(jnp.int32)
    ng = (cnt + GRAN - 1) // GRAN
    gl = jnp.cumsum(ng, axis=1) - ng
    ngtot = jnp.sum(ng, axis=1)
    tot_g = jnp.sum(ng, axis=0)
    tm_g = EXP_TM // GRAN
    region_g = -(-tot_g // tm_g) * tm_g
    goff_g = jnp.cumsum(region_g) - region_g
    run_g = goff_g[None, :] + jnp.cumsum(ng, axis=0) - ng
    g_idx = jnp.arange(MAX_GRAN + 1, dtype=jnp.int32)
    owner = jnp.sum(g_idx[None, :, None] >= (gl + ng)[:, None, :], axis=2)
    owner = jnp.minimum(owner, N_EXPERTS - 1).astype(jnp.int32)
    is_owner = owner[:, :, None] == jnp.arange(N_EXPERTS, dtype=jnp.int32)[None, None, :]
    dst_g = jnp.sum(jnp.where(is_owner, (run_g - gl)[:, None, :], 0), axis=2) + g_idx[None, :]
    gdst = (dst_g * GRAN).astype(jnp.int32).reshape(-1)
    n_et = _moe_rows(t) // EXP_TM
    tstart_g = jnp.arange(n_et, dtype=jnp.int32) * tm_g
    te = jnp.sum(tstart_g[:, None] >= (goff_g + region_g)[None, :], axis=1)
    tec = jnp.minimum(te, N_EXPERTS - 1).astype(jnp.int32)
    valid = jnp.clip((goff_g[tec] + tot_g[tec] - tstart_g) * GRAN, 0, EXP_TM)
    valid = jnp.where(te >= N_EXPERTS, 0, valid).astype(jnp.int32)
    locoff = jnp.broadcast_to((gl * GRAN).astype(F32)[:, :, None],
                              (cnt.shape[0], N_EXPERTS, MOE_TILE))
    ztail = jnp.concatenate([(goff_g + tot_g) * GRAN, region_g - tot_g,
                             jnp.sum(region_g, keepdims=True) // tm_g]).astype(jnp.int32)
    return ngtot.astype(jnp.int32), gdst, tec, valid, locoff, ztail


def _sorted_slots(rinfo_ref, locoff_ref, upper_ref):
    rt = jnp.transpose(rinfo_ref[...])
    i1, i2, w1, w2 = rt[0:1, :], rt[1:2, :], rt[2:3, :], rt[3:4, :]
    e_iota = lax.broadcasted_iota(jnp.int32, (N_EXPERTS, MOE_TILE), 0).astype(F32)
    a1 = e_iota == i1
    a2 = e_iota == i2
    a_t = jnp.where(a1, 1.0, jnp.where(a2, 1.0, 0.0)).astype(BF16)
    rank = jnp.dot(a_t, upper_ref[...], preferred_element_type=F32)
    pos = locoff_ref[0] + rank
    d1 = jnp.sum(jnp.where(a1, pos, 0.0), axis=0, keepdims=True)
    d2 = jnp.sum(jnp.where(a2, pos, 0.0), axis=0, keepdims=True)
    r_iota = lax.broadcasted_iota(jnp.int32, (SORT_ROWS, MOE_TILE), 0).astype(F32)
    return r_iota, d1, d2, w1, w2


def _granule_copy(src, dst, sem):
    return pltpu.make_async_copy(src, dst, sem)


def _dispatch_kernel(ngtot_s, gdst_s, ztail_s, h_ref, rinfo_ref, locoff_ref, upper_ref, xs_hbm,
                     buf, zbuf, sem):
    i = pl.program_id(0)
    last = pl.num_programs(0) - 1
    slot = lax.rem(i, 2)
    n_et = xs_hbm.shape[0] // EXP_TM

    def zero_tails(wait):
        def go(cp):
            cp.wait() if wait else cp.start()

        for e in range(N_EXPERTS):
            def gran(g, c, e=e):
                row = pl.multiple_of(ztail_s[e] + g * GRAN, GRAN)
                go(_granule_copy(zbuf.at[pl.ds(0, GRAN), :], xs_hbm.at[pl.ds(row, GRAN), :],
                                 sem.at[2]))
                return c
            lax.fori_loop(0, ztail_s[N_EXPERTS + e], gran, 0)

        def tile(tl, c):
            row = pl.multiple_of(tl * EXP_TM, EXP_TM)
            go(_granule_copy(zbuf, xs_hbm.at[pl.ds(row, EXP_TM), :], sem.at[2]))
            return c
        lax.fori_loop(ztail_s[2 * N_EXPERTS], n_et, tile, 0)

    @pl.when(i == 0)
    def _():
        zbuf[...] = jnp.zeros_like(zbuf)
        zero_tails(wait=False)

    def copy(tile, g, sl):
        src = buf.at[sl, pl.ds(pl.multiple_of(g * GRAN, GRAN), GRAN), :]
        row = pl.multiple_of(gdst_s[tile * (MAX_GRAN + 1) + g], GRAN)
        return _granule_copy(src, xs_hbm.at[pl.ds(row, GRAN), :], sem.at[sl])

    def wait_tile(tile, sl):
        def body(g, c):
            copy(tile, g, sl).wait()
            return c
        lax.fori_loop(0, ngtot_s[tile], body, 0)

    @pl.when(i >= 2)
    def _():
        wait_tile(i - 2, slot)

    r_iota, d1, d2, _, _ = _sorted_slots(rinfo_ref, locoff_ref, upper_ref)
    sel = jnp.where(r_iota == d1, 1.0, jnp.where(r_iota == d2, 1.0, 0.0)).astype(BF16)
    buf[slot] = jnp.dot(sel, h_ref[...], preferred_element_type=F32).astype(BF16)

    def start(g, c):
        copy(i, g, slot).start()
        return c
    lax.fori_loop(0, ngtot_s[i], start, 0)

    @pl.when(i == last)
    def _():
        @pl.when(i >= 1)
        def _():
            wait_tile(i - 1, 1 - slot)
        wait_tile(i, slot)
        zero_tails(wait=True)


def _dispatch(plan, h2, rinfo, upper):
    ngtot, gdst, _, _, locoff, ztail = plan
    t, d = h2.shape
    n_tiles = t // MOE_TILE
    grid_spec = pltpu.PrefetchScalarGridSpec(
        num_scalar_prefetch=3,
        grid=(n_tiles,),
        in_specs=[
            pl.BlockSpec((MOE_TILE, d), lambda i, *_: (i, 0)),
            pl.BlockSpec((MOE_TILE, LANES), lambda i, *_: (i, 0)),
            pl.BlockSpec((1, N_EXPERTS, MOE_TILE), lambda i, *_: (i, 0, 0)),
            pl.BlockSpec((MOE_TILE, MOE_TILE), lambda i, *_: (0, 0)),
        ],
        out_specs=pl.BlockSpec(memory_space=pl.ANY),
        scratch_shapes=[pltpu.VMEM((2, SORT_ROWS, d), BF16), pltpu.VMEM((EXP_TM, d), BF16),
                        pltpu.SemaphoreType.DMA((3,))],
    )
    return pl.pallas_call(
        _dispatch_kernel,
        grid_spec=grid_spec,
        out_shape=jax.ShapeDtypeStruct((_moe_rows(t), d), BF16),
        compiler_params=pltpu.CompilerParams(
            dimension_semantics=("arbitrary",), vmem_limit_bytes=VMEM_LIMIT),
        name="moe_dispatch",
    )(ngtot, gdst, ztail, h2, rinfo, locoff, upper)


def _expert_kernel(te_s, valid_s, x_ref, wg_ref, wu_ref, wd_ref, y_ref):
    valid = valid_s[pl.program_id(0)]

    @pl.when(valid > 0)
    def _():
        x = x_ref[...]
        a = jnp.dot(x, wg_ref[0].astype(BF16), preferred_element_type=F32)
        u = jnp.dot(x, wu_ref[0].astype(BF16), preferred_element_type=F32)
        hid = ((a * _sigmoid(a)) * u).astype(BF16)
        y_ref[...] = jnp.dot(hid, wd_ref[0].astype(BF16),
                             preferred_element_type=F32).astype(y_ref.dtype)

    @pl.when(valid <= 0)
    def _():
        y_ref[...] = jnp.zeros_like(y_ref)


def _experts(plan, xs, w_gate, w_up, w_down, layer):
    _, _, tec, valid, _, _ = plan
    rows, d = xs.shape
    _, _, _, ff = w_gate.shape
    grid_spec = pltpu.PrefetchScalarGridSpec(
        num_scalar_prefetch=2,
        grid=(rows // EXP_TM,),
        in_specs=[
            pl.BlockSpec((EXP_TM, d), lambda i, te, va: (i, 0)),
            pl.BlockSpec((None, 1, d, ff), lambda i, te, va: (layer, te[i], 0, 0)),
            pl.BlockSpec((None, 1, d, ff), lambda i, te, va: (layer, te[i], 0, 0)),
            pl.BlockSpec((None, 1, ff, d), lambda i, te, va: (layer, te[i], 0, 0)),
        ],
        out_specs=pl.BlockSpec((EXP_TM, d), lambda i, te, va: (i, 0)),
    )
    return pl.pallas_call(
        _expert_kernel,
        grid_spec=grid_spec,
        out_shape=jax.ShapeDtypeStruct((rows, d), BF16),
        compiler_params=pltpu.CompilerParams(
            dimension_semantics=("arbitrary",), vmem_limit_bytes=VMEM_LIMIT),
        name="moe_experts",
    )(tec, valid, xs, w_gate, w_up, w_down)


def _combine_kernel(ngtot_s, gdst_s, ys_hbm, rinfo_ref, locoff_ref, upper_ref, x1_ref, mod_ref,
                    lng_ref, lnb_ref, o_ref, buf, sem):
    i = pl.program_id(0)
    last = pl.num_programs(0) - 1
    slot = lax.rem(i, 2)

    def copy(tile, g, sl):
        row = pl.multiple_of(gdst_s[tile * (MAX_GRAN + 1) + g], GRAN)
        dst = buf.at[sl, pl.ds(pl.multiple_of(g * GRAN, GRAN), GRAN), :]
        return _granule_copy(ys_hbm.at[pl.ds(row, GRAN), :], dst, sem.at[sl])

    def start_tile(tile, sl):
        def body(g, c):
            copy(tile, g, sl).start()
            return c
        lax.fori_loop(0, ngtot_s[tile], body, 0)

    @pl.when(i == 0)
    def _():
        buf[...] = jnp.zeros_like(buf)
        start_tile(0, 0)

    @pl.when(i < last)
    def _():
        start_tile(i + 1, 1 - slot)

    def wait(g, c):
        copy(i, g, slot).wait()
        return c
    lax.fori_loop(0, ngtot_s[i], wait, 0)

    r_iota, d1, d2, w1, w2 = _sorted_slots(rinfo_ref, locoff_ref, upper_ref)
    wsel = jnp.where(r_iota == d1, w1, jnp.where(r_iota == d2, w2, 0.0)).astype(BF16)
    moe = lax.dot_general(wsel, buf[slot], (((0,), (0,)), ((), ())),
                          preferred_element_type=F32)
    gate2 = mod_ref[5:6, :]
    xn = _layer_norm_rows(DEEPNORM_ALPHA * x1_ref[...] + (1.0 + gate2) * moe)
    o_ref[...] = xn * lng_ref[...] + lnb_ref[...]


def _combine(plan, ys, rinfo, upper, x1, mod_l, ln_g, ln_b, seq):
    ngtot, gdst, _, _, locoff, _ = plan
    t, d = x1.shape
    tiles_per_batch = seq // MOE_TILE
    grid_spec = pltpu.PrefetchScalarGridSpec(
        num_scalar_prefetch=2,
        grid=(t // MOE_TILE,),
        in_specs=[
            pl.BlockSpec(memory_space=pl.ANY),
            pl.BlockSpec((MOE_TILE, LANES), lambda i, *_: (i, 0)),
            pl.BlockSpec((1, N_EXPERTS, MOE_TILE), lambda i, *_: (i, 0, 0)),
            pl.BlockSpec((MOE_TILE, MOE_TILE), lambda i, *_: (0, 0)),
            pl.BlockSpec((MOE_TILE, d), lambda i, *_: (i, 0)),
            pl.BlockSpec((None, N_MOD, d), lambda i, *_: (i // tiles_per_batch, 0, 0)),
            pl.BlockSpec((1, d), lambda i, *_: (0, 0)),
            pl.BlockSpec((1, d), lambda i, *_: (0, 0)),
        ],
        out_specs=pl.BlockSpec((MOE_TILE, d), lambda i, *_: (i, 0)),
        scratch_shapes=[pltpu.VMEM((2, SORT_ROWS, d), BF16), pltpu.SemaphoreType.DMA((2,))],
    )
    return pl.pallas_call(
        _combine_kernel,
        grid_spec=grid_spec,
        out_shape=jax.ShapeDtypeStruct((t, d), F32),
        compiler_params=pltpu.CompilerParams(
            dimension_semantics=("arbitrary",), vmem_limit_bytes=VMEM_LIMIT),
        name="moe_combine",
    )(ngtot, gdst, ys, rinfo, locoff, upper, x1, mod_l, ln_g, ln_b)


def _pad_lanes(a):
    return jnp.pad(a, ((0, 0), (0, LANES - a.shape[1])))


def kernel(x, c, w_mod, b_mod, w_in, conv_w, conv_b, dt_bias, a_log, d_skip, ssd_norm_w, forget_b,
           w_ssd_o, w_att_o, w_o, ln1_g, ln1_b, w_router_group, b_router_group, w_router_expert,
           b_router_expert, w_gate, w_up, w_down, ln2_g, ln2_b):
    bsz, seq, d = x.shape
    depth = w_mod.shape[0]
    t = bsz * seq
    assert seq % MOE_TILE == 0 and d == SSD_INNER
    tm_proj = min(512, seq)
    ssd_rows = min(512, seq)
    upper = jnp.asarray(np.triu(np.ones((MOE_TILE, MOE_TILE), np.float32), k=1), BF16)
    tq = min(512, seq)
    tk = min(512, seq)

    mod = _modulation(c, w_mod, b_mod).reshape(depth, bsz, N_MOD, d)
    consts = _ssd_constants()
    placement = _attn_placement()
    in_sizes = (SSD_INNER, SSD_CONV_CH, SSD_HEADS, ATT_INNER, ATT_INNER, ATT_INNER, ATT_HEADS, d, d)
    offs = np.concatenate([[0], np.cumsum(in_sizes)]).tolist()

    x2 = x.reshape(t, d)
    for l in range(depth):
        cols = [w_in[l][:, offs[i]:offs[i + 1]] for i in range(len(in_sizes))]
        wz, wxbc, wdt, wq, wk, wv, wf, wgs, wga = cols
        wq = wq * (math.log2(math.e) * ATT_HEAD_DIM ** -0.5)
        big = [w.astype(BF16) for w in (wz, wxbc, wq, wk, wv, wgs, wga)]
        w_small = _pad_lanes(jnp.concatenate([wdt, wf], axis=1))
        z, xbc, q, k, v, gs, ga, small = _inproj(x2, mod[l], big, w_small, seq, tm_proj)

        hvec = jnp.concatenate([
            _pad_lanes(jnp.concatenate([dt_bias[l], forget_b[l]])[None, :]),
            _pad_lanes(a_log[l][None, :]),
            jnp.zeros((6, LANES), F32)], axis=0)
        dskip_x = jnp.repeat(d_skip[l], SSD_HEAD_DIM)[None, :]
        y, fcum = _ssd(xbc.reshape(bsz, seq, SSD_CONV_CH), small.reshape(bsz, seq, LANES),
                       conv_w[l], conv_b[l][None, :], hvec, dskip_x, consts, ssd_rows)

        o = _attention(q.reshape(bsz, seq, ATT_INNER), k.reshape(bsz, seq, ATT_INNER),
                       v.reshape(bsz, seq, ATT_INNER), fcum, placement, tq, tk)

        w_r = _pad_lanes(jnp.concatenate([w_router_expert[l], w_router_group[l]], axis=1))
        b_r = _pad_lanes(jnp.concatenate([b_router_expert[l], b_router_group[l]])[None, :])
        x1, h2, rinfo, cnt = _mixout(
            y.reshape(t, SSD_INNER), z, o.reshape(t, ATT_INNER), gs, ga, x2, mod[l],
            ssd_norm_w[l][None, :], w_ssd_o[l].astype(BF16), w_att_o[l].astype(BF16),
            w_o[l].astype(BF16), ln1_g[l][None, :], ln1_b[l][None, :], w_r, b_r, seq, MOE_TILE)

        plan = _moe_plan(cnt[:, 0, :N_EXPERTS], t)
        xs = _dispatch(plan, h2, rinfo, upper)
        ys = _experts(plan, xs, w_gate, w_up, w_down, l)
        x2 = _combine(plan, ys, rinfo, upper, x1, mod[l], ln2_g[l][None, :], ln2_b[l][None, :],
                      seq)
    return x2.reshape(bsz, seq, d)
```

```python
import functools
import math

import numpy as np
import jax
import jax.numpy as jnp
from jax import lax
from jax.experimental import pallas as pl
from jax.experimental.pallas import tpu as pltpu

SSD_HEAD_DIM = 64
SSD_HEADS = 16
SSD_GROUPS = 4
SSD_STATE = 128
SSD_CHUNK = 128
SSD_CONV = 4
SSD_INNER = SSD_HEADS * SSD_HEAD_DIM
SSD_BC = SSD_GROUPS * SSD_STATE
SSD_CONV_CH = SSD_INNER + 2 * SSD_BC
ATT_HEAD_DIM = 64
ATT_HEADS = 8
ATT_INNER = ATT_HEADS * ATT_HEAD_DIM
N_GROUPS = 4
EXPERTS_PER_GROUP = 4
N_EXPERTS = 16
N_MOD = 6
DEPTH_FOR_DEEPNORM = 4
DEEPNORM_ALPHA = (2 * DEPTH_FOR_DEEPNORM) ** 0.25
LN_EPS = 1e-5

LANES = 128
VMEM_LIMIT = 56 * 1024 * 1024

F32 = jnp.float32
BF16 = jnp.bfloat16


def _sigmoid(x):
    return 1.0 / (1.0 + jnp.exp2(x * (-math.log2(math.e))))


def _layer_norm_rows(x):
    mu = jnp.mean(x, axis=-1, keepdims=True)
    xc = x - mu
    var = jnp.mean(xc * xc, axis=-1, keepdims=True)
    return xc * lax.rsqrt(var + LN_EPS)


def _dot_split(h, h_hi, w):
    h_lo = (h - h_hi.astype(F32)).astype(BF16)
    w_hi = w.astype(BF16)
    w_lo = (w - w_hi.astype(F32)).astype(BF16)
    r = jnp.dot(h_hi, jnp.concatenate([w_hi, w_lo], axis=1), preferred_element_type=F32)
    return r[:, :LANES] + r[:, LANES:] + jnp.dot(h_lo, w_hi, preferred_element_type=F32)


def _const_spec(shape):
    n = len(shape)
    return pl.BlockSpec(shape, lambda *_: (0,) * n)


def _mod_kernel(c_ref, w_ref, b_ref, o_ref):
    c = c_ref[...]
    ca = c * _sigmoid(c)
    ca_hi = ca.astype(BF16)
    ca_lo = (ca - ca_hi.astype(F32)).astype(BF16)
    w = w_ref[0]
    w_hi = w.astype(BF16)
    w_lo = (w - w_hi.astype(F32)).astype(BF16)
    o_ref[0] = (jnp.dot(ca_hi, w_hi, preferred_element_type=F32)
                + jnp.dot(ca_hi, w_lo, preferred_element_type=F32)
                + jnp.dot(ca_lo, w_hi, preferred_element_type=F32) + b_ref[0])


def _modulation(c, w_mod, b_mod):
    depth, d, n = w_mod.shape
    bsz = c.shape[0]
    tn = 1024
    return pl.pallas_call(
        _mod_kernel,
        grid=(depth, n // tn),
        in_specs=[
            pl.BlockSpec((bsz, d), lambda l, j: (0, 0)),
            pl.BlockSpec((1, d, tn), lambda l, j: (l, 0, j)),
            pl.BlockSpec((1, 1, tn), lambda l, j: (l, 0, j)),
        ],
        out_specs=pl.BlockSpec((1, bsz, tn), lambda l, j: (l, 0, j)),
        out_shape=jax.ShapeDtypeStruct((depth, bsz, n), F32),
        compiler_params=pltpu.CompilerParams(
            dimension_semantics=("arbitrary", "arbitrary"), vmem_limit_bytes=VMEM_LIMIT),
        name="modulation",
    )(c, w_mod, b_mod.reshape(depth, 1, n))


def _inproj_kernel(x_ref, mod_ref, wz_ref, wxbc_ref, wq_ref, wk_ref, wv_ref, wgs_ref, wga_ref,
                   wsm_ref, z_ref, xbc_ref, q_ref, k_ref, v_ref, gs_ref, ga_ref, sm_ref):
    x = x_ref[...]
    shift = mod_ref[0:1, :]
    scale = mod_ref[1:2, :]
    h = _layer_norm_rows(x) * (1.0 + scale) + shift
    hb = h.astype(BF16)
    for w_ref, o_ref in ((wz_ref, z_ref), (wxbc_ref, xbc_ref), (wq_ref, q_ref), (wk_ref, k_ref),
                         (wv_ref, v_ref), (wgs_ref, gs_ref), (wga_ref, ga_ref)):
        o_ref[...] = jnp.dot(hb, w_ref[...], preferred_element_type=F32).astype(o_ref.dtype)
    sm_ref[...] = _dot_split(h, hb, wsm_ref[...])


def _inproj(x2, mod_l, weights, w_small, seq, tm):
    t, d = x2.shape
    tiles_per_batch = seq // tm
    widths = [w.shape[1] for w in weights]
    in_specs = [
        pl.BlockSpec((tm, d), lambda i: (i, 0)),
        pl.BlockSpec((None, N_MOD, d), lambda i: (i // tiles_per_batch, 0, 0)),
    ]
    in_specs += [_const_spec(w.shape) for w in weights]
    in_specs += [_const_spec(w_small.shape)]
    out_specs = [pl.BlockSpec((tm, n), lambda i: (i, 0)) for n in widths]
    out_specs += [pl.BlockSpec((tm, LANES), lambda i: (i, 0))]
    out_shape = [jax.ShapeDtypeStruct((t, n), BF16) for n in widths]
    out_shape += [jax.ShapeDtypeStruct((t, LANES), F32)]
    return pl.pallas_call(
        _inproj_kernel,
        grid=(t // tm,),
        in_specs=in_specs,
        out_specs=out_specs,
        out_shape=out_shape,
        compiler_params=pltpu.CompilerParams(
            dimension_semantics=("arbitrary",), vmem_limit_bytes=VMEM_LIMIT),
        name="inproj",
    )(x2, mod_l, *weights, w_small)


CONV_HIST = 16
CONV_WIN = CONV_HIST + SSD_CHUNK


def _ssd_constants():
    L = SSD_CHUNK
    sh = np.zeros((L, SSD_CONV * CONV_WIN), np.float32)
    for j in range(SSD_CONV):
        for t in range(L):
            sh[t, j * CONV_WIN + CONV_HIST + t - j] = 1.0
    tril = np.tril(np.ones((L, L), np.float32))
    e3 = np.zeros((LANES, SSD_INNER), np.float32)
    for piece in range(3):
        for h in range(SSD_HEADS):
            e3[32 * piece + h, h * SSD_HEAD_DIM:(h + 1) * SSD_HEAD_DIM] = 1.0
    r = SSD_HEADS // SSD_GROUPS
    bd = np.zeros((r * L, r * SSD_HEAD_DIM), np.float32)
    for i in range(r):
        bd[i * L:(i + 1) * L, i * SSD_HEAD_DIM:(i + 1) * SSD_HEAD_DIM] = 1.0
    return (jnp.asarray(sh, BF16), jnp.asarray(tril, BF16), jnp.asarray(e3, BF16),
            jnp.asarray(bd, BF16))


def _split3_bf16(a):
    hi = a.astype(BF16)
    r1 = a - hi.astype(F32)
    mid = r1.astype(BF16)
    lo = (r1 - mid.astype(F32)).astype(BF16)
    return hi, mid, lo


def _expand_heads(xm, e3):
    hi, mid, lo = _split3_bf16(xm)
    packed = (hi.astype(F32) + pltpu.roll(mid.astype(F32), 32, 1)
              + pltpu.roll(lo.astype(F32), 64, 1)).astype(BF16)
    return jnp.dot(packed, e3, preferred_element_type=F32)


def _ssd_kernel(xbc_ref, sm_ref, cw_ref, cb_ref, hv_ref, dskip_ref, sh_ref, tril_ref, e3_ref,
                bd_ref, y_ref, fcum_ref, uext, state, fcarry, *, rows):
    L = SSD_CHUNK
    j = pl.program_id(1)

    @pl.when(j == 0)
    def _():
        uext[0:CONV_HIST, :] = jnp.zeros((CONV_HIST, SSD_CONV_CH), BF16)
        state[...] = jnp.zeros_like(state)
        fcarry[...] = jnp.zeros_like(fcarry)

    uext[CONV_HIST:CONV_HIST + rows, :] = xbc_ref[0]
    cw_b = cw_ref[...].astype(BF16)

    lane = lax.broadcasted_iota(jnp.int32, (1, LANES), 1)
    is_dt = lane < SSD_HEADS
    is_f = jnp.logical_and(lane >= SSD_HEADS, lane < SSD_HEADS + ATT_HEADS)
    a_neg = -jnp.exp(hv_ref[1:2, :])
    bias = hv_ref[0:1, :]
    rr = lax.broadcasted_iota(jnp.int32, (L, L), 0)
    cc = lax.broadcasted_iota(jnp.int32, (L, L), 1)
    causal = rr >= cc
    gw = (SSD_HEADS // SSD_GROUPS) * SSD_HEAD_DIM

    for c in range(rows // L):
        r0 = c * L
        win = uext[r0:r0 + CONV_WIN, :]
        taps = jnp.concatenate([win * cw_b[SSD_CONV - 1 - j:SSD_CONV - j, :]
                                for j in range(SSD_CONV)], axis=0)
        conv = jnp.dot(sh_ref[...], taps, preferred_element_type=F32) + cb_ref[...]
        act = conv * _sigmoid(conv)
        xs = act[:, :SSD_INNER]
        bm = act[:, SSD_INNER:SSD_INNER + SSD_BC]
        cm = act[:, SSD_INNER + SSD_BC:]

        pre = sm_ref[0, r0:r0 + L, :] + bias
        tail = jnp.log(1.0 + jnp.exp(-jnp.abs(pre)))
        dt = jnp.where(is_dt, jnp.maximum(pre, 0.0) + tail, 0.0)
        log_f = jnp.minimum(pre, 0.0) - tail
        comb = jnp.where(is_dt, dt * a_neg, jnp.where(is_f, log_f, 0.0))
        c3 = jnp.dot(tril_ref[...], jnp.concatenate(_split3_bf16(comb), axis=1),
                     preferred_element_type=F32)
        cs = c3[:, :LANES] + c3[:, LANES:2 * LANES] + c3[:, 2 * LANES:]
        fc = cs + fcarry[...]
        fcarry[...] = jnp.where(is_f, fc[L - 1:L, :], 0.0)
        f_hi, f_mid, f_lo = _split3_bf16(jnp.where(is_f, fc * math.log2(math.e), 0.0))
        fcum_ref[0, r0:r0 + L, :] = (
            f_hi.astype(F32) + pltpu.roll(f_mid.astype(F32), 32, 1)
            + pltpu.roll(f_lo.astype(F32), 64, 1)
            + jnp.where(lane == LANES - 1, 1.0, 0.0)).astype(BF16)

        a_cs = jnp.where(is_dt, cs, 0.0)
        a_cs_x = _expand_heads(a_cs, e3_ref[...])
        dt_x = _expand_heads(dt, e3_ref[...])
        a_last_x = a_cs_x[L - 1:L, :]
        xdt = xs * dt_x
        xdt_b = xdt.astype(BF16)
        xend_b = (xdt * jnp.exp(a_last_x - a_cs_x)).astype(BF16)
        e_acs = jnp.exp(a_cs_x)
        e_last = jnp.exp(a_last_x)
        a_cs_t = jnp.transpose(a_cs)

        y_groups = []
        for g in range(SSD_GROUPS):
            bg = bm[:, g * SSD_STATE:(g + 1) * SSD_STATE]
            cg_b = cm[:, g * SSD_STATE:(g + 1) * SSD_STATE].astype(BF16)
            cb = lax.dot_general(cg_b, bg.astype(BF16), (((1,), (1,)), ((), ())),
                                 preferred_element_type=F32)
            ms = []
            for r in range(SSD_HEADS // SSD_GROUPS):
                h = g * (SSD_HEADS // SSD_GROUPS) + r
                seg = a_cs[:, h:h + 1] - a_cs_t[h:h + 1, :]
                dec = jnp.exp(jnp.where(causal, seg, -jnp.inf))
                ms.append((cb * dec).astype(BF16))
            m_cat = jnp.concatenate(ms, axis=1)
            xg = xdt_b[:, g * gw:(g + 1) * gw]
            x_bd = jnp.concatenate([xg] * (SSD_HEADS // SSD_GROUPS), axis=0) * bd_ref[...]
            y_diag = jnp.dot(m_cat, x_bd, preferred_element_type=F32)
            st = state[:, g * gw:(g + 1) * gw]
            y_off = jnp.dot(cg_b, st.astype(BF16), preferred_element_type=F32)
            y_groups.append(y_diag + y_off * e_acs[:, g * gw:(g + 1) * gw])
            new_st = jnp.dot(jnp.transpose(bg).astype(BF16), xend_b[:, g * gw:(g + 1) * gw],
                             preferred_element_type=F32)
            state[:, g * gw:(g + 1) * gw] = st * e_last[:, g * gw:(g + 1) * gw] + new_st
        y = jnp.concatenate(y_groups, axis=1) + dskip_ref[...] * xs
        y_ref[0, r0:r0 + L, :] = y.astype(y_ref.dtype)

    uext[0:CONV_HIST, :] = uext[rows:rows + CONV_HIST, :]


def _ssd(xbc, small, conv_w, conv_b, hvec, dskip_x, consts, rows):
    bsz, seq, _ = xbc.shape
    sh, tril, e3, bd = consts
    kern = functools.partial(_ssd_kernel, rows=rows)
    return pl.pallas_call(
        kern,
        grid=(bsz, seq // rows),
        in_specs=[
            pl.BlockSpec((1, rows, SSD_CONV_CH), lambda b, j: (b, j, 0)),
            pl.BlockSpec((1, rows, LANES), lambda b, j: (b, j, 0)),
            _const_spec(conv_w.shape), _const_spec(conv_b.shape), _const_spec(hvec.shape),
            _const_spec(dskip_x.shape), _const_spec(sh.shape), _const_spec(tril.shape),
            _const_spec(e3.shape), _const_spec(bd.shape),
        ],
        out_specs=[
            pl.BlockSpec((1, rows, SSD_INNER), lambda b, j: (b, j, 0)),
            pl.BlockSpec((1, rows, LANES), lambda b, j: (b, j, 0)),
        ],
        out_shape=[
            jax.ShapeDtypeStruct((bsz, seq, SSD_INNER), BF16),
            jax.ShapeDtypeStruct((bsz, seq, LANES), BF16),
        ],
        scratch_shapes=[
            pltpu.VMEM((CONV_HIST + rows, SSD_CONV_CH), BF16),
            pltpu.VMEM((SSD_STATE, SSD_INNER), F32),
            pltpu.VMEM((1, LANES), F32),
        ],
        compiler_params=pltpu.CompilerParams(
            dimension_semantics=("arbitrary", "arbitrary"), vmem_limit_bytes=VMEM_LIMIT),
        name="ssd",
    )(xbc, small, conv_w, conv_b, hvec, dskip_x, sh, tril, e3, bd)


def _attn_placement():
    pq = np.zeros((ATT_HEADS // 2, LANES, 2 * LANES), np.float32)
    pk = np.zeros_like(pq)
    for p in range(ATT_HEADS // 2):
        for hd in range(2):
            h = 2 * p + hd
            col = hd * LANES + (ATT_HEAD_DIM if hd == 0 else 0)
            for piece in range(3):
                pq[p, 32 * piece + SSD_HEADS + h, col + piece] = 1.0
                pq[p, LANES - 1, col + 3 + piece] = 1.0
                pk[p, LANES - 1, col + piece] = 1.0
                pk[p, 32 * piece + SSD_HEADS + h, col + 3 + piece] = -1.0
    return jnp.asarray(pq, BF16), jnp.asarray(pk, BF16)


def _attn_kernel(q_ref, k_ref, v_ref, fq_ref, fk_ref, pq_ref, pk_ref, o_ref, kaug_sc, s_sc,
                 m_sc, acc_sc, *, tq, tk):
    qi = pl.program_id(2)
    seq = k_ref.shape[1]
    r = tq // tk
    n_full = qi * r
    key_idx = lax.broadcasted_iota(jnp.int32, (tk, tq), 0)
    qry_idx = lax.broadcasted_iota(jnp.int32, (tk, tq), 1)
    lane2 = lax.broadcasted_iota(jnp.int32, (1, 2 * LANES), 1)
    keep = jnp.logical_or(lane2 < ATT_HEAD_DIM, lane2 >= LANES + ATT_HEAD_DIM)

    def augment(x, f, place):
        aug = jnp.dot(f, place, preferred_element_type=F32).astype(BF16)
        return jnp.where(keep, jnp.concatenate([x, x], axis=1), aug)

    npair = q_ref.shape[2] // LANES
    heads = range(2 * npair)
    pair_cols = lambda pp: slice(pp * LANES, (pp + 1) * LANES)
    slab_cols = lambda h: slice(h * LANES, (h + 1) * LANES)

    @pl.when(qi == 0)
    def _():
        def fill(c, carry):
            rows = pl.ds(pl.multiple_of(c * tk, tk), tk)
            for pp in range(npair):
                kaug_sc[rows, pp * 2 * LANES:(pp + 1) * 2 * LANES] = augment(
                    k_ref[0, rows, pair_cols(pp)], fk_ref[0, rows, :], pk_ref[pp])
            return carry
        lax.fori_loop(0, seq // tk, fill, 0)

    q_aug_t = []
    for pp in range(npair):
        qa = augment(q_ref[0, :, pair_cols(pp)], fq_ref[0], pq_ref[pp]).astype(F32)
        q_aug_t += [jnp.transpose(qa[:, slab_cols(hd)]).astype(BF16) for hd in range(2)]

    def scores(kblk):
        start = pl.multiple_of(kblk * tk, tk)
        kb = kaug_sc[pl.ds(start, tk), :]
        return [jnp.dot(kb[:, slab_cols(h)], q_aug_t[h], preferred_element_type=F32)
                for h in heads]

    def prepare(s_list, diag):
        mcols = []
        for h in heads:
            s_t = s_list[h]
            if diag is not None:
                s_t = jnp.where(key_idx + diag * tk <= qry_idx, s_t, -jnp.inf)
            s_sc[h] = s_t
            mcols.append(jnp.max(s_t, axis=0, keepdims=True))
        return mcols

    def consume(kblk, mcols):
        start = pl.multiple_of(kblk * tk, tk)
        vb = v_ref[0, pl.ds(start, tk), :]
        for h in heads:
            m_old = m_sc[h]
            m_new = jnp.maximum(m_old, mcols[h])
            p_t = jnp.exp2((s_sc[h] - m_new).astype(BF16))
            alpha = jnp.exp2(m_old - m_new)
            vb_p = vb[:, pair_cols(h // 2)]
            vb_h = jnp.where(own_lanes[h % 2], vb_p, jnp.ones_like(vb_p))
            pv = lax.dot_general(vb_h, p_t, (((0,), (0,)), ((), ())),
                                 preferred_element_type=F32)
            acc_sc[h] = alpha * acc_sc[h] + pv
            m_sc[h] = m_new

    lane1 = lax.broadcasted_iota(jnp.int32, (1, LANES), 1)
    own_lanes = (lane1 < ATT_HEAD_DIM, lane1 >= ATT_HEAD_DIM)
    m_sc[...] = jnp.full_like(m_sc, -jnp.inf)
    acc_sc[...] = jnp.zeros_like(acc_sc)
    mcols = prepare(scores(n_full), 0)
    for d in range(r):
        s_next = scores(n_full + d + 1 if d + 1 < r else 0)
        consume(n_full + d, mcols)
        mcols = prepare(s_next, d + 1 if d + 1 < r else None)

    def body(jb, mcols):
        s_next = scores(jnp.minimum(jb + 1, n_full - 1))
        consume(jb, mcols)
        return tuple(prepare(s_next, None))

    lax.fori_loop(0, n_full, body, tuple(mcols))
    hdim = ATT_HEAD_DIM
    for pp in range(npair):
        acc0, acc1 = acc_sc[2 * pp], acc_sc[2 * pp + 1]
        o_t = jnp.concatenate([acc0[:hdim] / acc0[hdim:hdim + 1], acc1[hdim:] / acc1[0:1]],
                              axis=0)
        o_ref[0, :, pair_cols(pp)] = jnp.transpose(o_t).astype(o_ref.dtype)


ATT_PAIRS_PER_STEP = 4


def _attention(q, k, v, f_pieces, placement, tq, tk):
    bsz, seq, _ = q.shape
    npair = ATT_PAIRS_PER_STEP
    steps = ATT_HEADS // (2 * npair)
    pq, pk = placement
    kern = functools.partial(_attn_kernel, tq=tq, tk=tk)
    wide = npair * LANES
    return pl.pallas_call(
        kern,
        grid=(bsz, steps, seq // tq),
        in_specs=[
            pl.BlockSpec((1, tq, wide), lambda b, p, i: (b, i, p)),
            pl.BlockSpec((1, seq, wide), lambda b, p, i: (b, 0, p)),
            pl.BlockSpec((1, seq, wide), lambda b, p, i: (b, 0, p)),
            pl.BlockSpec((1, tq, LANES), lambda b, p, i: (b, i, 0)),
            pl.BlockSpec((1, seq, LANES), lambda b, p, i: (b, 0, 0)),
            pl.BlockSpec((npair, LANES, 2 * LANES), lambda b, p, i: (p, 0, 0)),
            pl.BlockSpec((npair, LANES, 2 * LANES), lambda b, p, i: (p, 0, 0)),
        ],
        out_specs=pl.BlockSpec((1, tq, wide), lambda b, p, i: (b, i, p)),
        out_shape=jax.ShapeDtypeStruct((bsz, seq, ATT_INNER), BF16),
        scratch_shapes=[pltpu.VMEM((seq, 2 * wide), BF16),
                        pltpu.VMEM((2 * npair, tk, tq), F32),
                        pltpu.VMEM((2 * npair, 1, tq), F32),
                        pltpu.VMEM((2 * npair, LANES, tq), F32)],
        compiler_params=pltpu.CompilerParams(
            dimension_semantics=("arbitrary", "arbitrary", "arbitrary"),
            vmem_limit_bytes=VMEM_LIMIT),
        name="fox_attention",
    )(q, k, v, f_pieces, f_pieces, pq, pk)


def _route(logits):
    lane = lax.broadcasted_iota(jnp.int32, logits.shape, 1)
    neg = -jnp.inf
    big = jnp.int32(1 << 20)
    is_g = jnp.logical_and(lane >= N_EXPERTS, lane < N_EXPERTS + N_GROUPS)
    gl = jnp.where(is_g, logits, neg)
    gmax = jnp.max(gl, axis=1, keepdims=True)
    gsum = jnp.sum(jnp.exp(gl - gmax), axis=1, keepdims=True)
    g_p = 1.0 / gsum
    g_idx = jnp.min(jnp.where(gl == gmax, lane, big), axis=1, keepdims=True) - N_EXPERTS
    lo = g_idx * EXPERTS_PER_GROUP
    in_group = jnp.logical_and(lane >= lo, lane < lo + EXPERTS_PER_GROUP)
    el = jnp.where(in_group, logits, neg)
    m1 = jnp.max(el, axis=1, keepdims=True)
    i1 = jnp.min(jnp.where(el == m1, lane, big), axis=1, keepdims=True)
    el2 = jnp.where(lane == i1, neg, el)
    m2 = jnp.max(el2, axis=1, keepdims=True)
    i2 = jnp.min(jnp.where(el2 == m2, lane, big), axis=1, keepdims=True)
    e2 = jnp.exp(m2 - m1)
    w1 = g_p / (1.0 + e2)
    w2 = g_p * e2 / (1.0 + e2)
    info = jnp.where(lane == 0, i1.astype(F32), jnp.where(lane == 1, i2.astype(F32),
                     jnp.where(lane == 2, w1, jnp.where(lane == 3, w2, 0.0))))
    chosen = jnp.where(lane == i1, 1.0, jnp.where(lane == i2, 1.0, 0.0))
    return info, jnp.sum(chosen, axis=0, keepdims=True)


def _mixout_kernel(y_ref, z_ref, o_ref, gs_ref, ga_ref, x_ref, mod_ref, nw_ref, wso_ref, wao_ref,
                   wo_ref, lng_ref, lnb_ref, wr_ref, br_ref, x1_ref, h2_ref, rinfo_ref, cnt_ref):
    tm = x_ref.shape[0]
    parts = 2
    rows_per = tm // parts
    counts = jnp.zeros((1, LANES), F32)
    for part in range(parts):
        rs = slice(part * rows_per, (part + 1) * rows_per)
        g = y_ref[rs, :].astype(F32)
        z = z_ref[rs, :].astype(F32)
        g = g * (z * _sigmoid(z))
        g = g * lax.rsqrt(jnp.mean(g * g, axis=-1, keepdims=True) + LN_EPS) * nw_ref[...]
        y_ssd = jnp.dot(g.astype(BF16), wso_ref[...], preferred_element_type=F32)
        y_att = jnp.dot(o_ref[rs, :], wao_ref[...], preferred_element_type=F32)
        merged = (_sigmoid(gs_ref[rs, :].astype(F32)) * y_ssd
                  + _sigmoid(ga_ref[rs, :].astype(F32)) * y_att)
        mix = jnp.dot(merged.astype(BF16), wo_ref[...], preferred_element_type=F32)
        gate1 = mod_ref[2:3, :]
        x1 = _layer_norm_rows(DEEPNORM_ALPHA * x_ref[rs, :] + (1.0 + gate1) * mix)
        x1 = x1 * lng_ref[...] + lnb_ref[...]
        x1_ref[rs, :] = x1
        h2 = _layer_norm_rows(x1) * (1.0 + mod_ref[4:5, :]) + mod_ref[3:4, :]
        h2b = h2.astype(BF16)
        h2_ref[rs, :] = h2b
        info, cnt = _route(_dot_split(h2, h2b, wr_ref[...]) + br_ref[...])
        rinfo_ref[rs, :] = info
        counts = counts + cnt
    cnt_ref[0] = jnp.broadcast_to(counts, (8, LANES))


def _mixout(y, z, o, gs, ga, x2, mod_l, norm_w, w_ssd_o, w_att_o, w_o, ln_g, ln_b, w_r, b_r,
            seq, tm):
    t, d = x2.shape
    tiles_per_batch = seq // tm
    row = lambda n: pl.BlockSpec((tm, n), lambda i: (i, 0))
    return pl.pallas_call(
        _mixout_kernel,
        grid=(t // tm,),
        in_specs=[
            row(SSD_INNER), row(SSD_INNER), row(ATT_INNER), row(d), row(d), row(d),
            pl.BlockSpec((None, N_MOD, d), lambda i: (i // tiles_per_batch, 0, 0)),
            _const_spec(norm_w.shape), _const_spec(w_ssd_o.shape), _const_spec(w_att_o.shape),
            _const_spec(w_o.shape), _const_spec(ln_g.shape), _const_spec(ln_b.shape),
            _const_spec(w_r.shape), _const_spec(b_r.shape),
        ],
        out_specs=[row(d), row(d), row(LANES), pl.BlockSpec((1, 8, LANES), lambda i: (i, 0, 0))],
        out_shape=[
            jax.ShapeDtypeStruct((t, d), F32),
            jax.ShapeDtypeStruct((t, d), BF16),
            jax.ShapeDtypeStruct((t, LANES), F32),
            jax.ShapeDtypeStruct((t // tm, 8, LANES), F32),
        ],
        compiler_params=pltpu.CompilerParams(
            dimension_semantics=("arbitrary",), vmem_limit_bytes=VMEM_LIMIT),
        name="mixer_out",
    )(y, z, o, gs, ga, x2, mod_l, norm_w, w_ssd_o, w_att_o, w_o, ln_g, ln_b, w_r, b_r)


MOE_TILE = 512
GRAN = 16
MAX_GRAN = 2 * MOE_TILE // GRAN + N_EXPERTS - 1
SORT_ROWS = (MAX_GRAN + 1) * GRAN
EXP_TM = 512


def _moe_rows(t):
    rows = 2 * t + (t // MOE_TILE) * N_EXPERTS * (GRAN - 1) + N_EXPERTS * EXP_TM
    return -(-rows // EXP_TM) * EXP_TM


def _moe_plan(cnt, t):
    cnt = cnt.astype(jnp.int32)
    ng = (cnt + GRAN - 1) // GRAN
    gl = jnp.cumsum(ng, axis=1) - ng
    ngtot = jnp.sum(ng, axis=1)
    tot_g = jnp.sum(ng, axis=0)
    tm_g = EXP_TM // GRAN
    region_g = -(-tot_g // tm_g) * tm_g
    goff_g = jnp.cumsum(region_g) - region_g
    run_g = goff_g[None, :] + jnp.cumsum(ng, axis=0) - ng
    g_idx = jnp.arange(MAX_GRAN + 1, dtype=jnp.int32)
    owner = jnp.sum(g_idx[None, :, None] >= (gl + ng)[:, None, :], axis=2)
    owner = jnp.minimum(owner, N_EXPERTS - 1).astype(jnp.int32)
    is_owner = owner[:, :, None] == jnp.arange(N_EXPERTS, dtype=jnp.int32)[None, None, :]
    dst_g = jnp.sum(jnp.where(is_owner, (run_g - gl)[:, None, :], 0), axis=2) + g_idx[None, :]
    gdst = (dst_g * GRAN).astype(jnp.int32).reshape(-1)
    n_et = _moe_rows(t) // EXP_TM
    tstart_g = jnp.arange(n_et, dtype=jnp.int32) * tm_g
    te = jnp.sum(tstart_g[:, None] >= (goff_g + region_g)[None, :], axis=1)
    tec = jnp.minimum(te, N_EXPERTS - 1).astype(jnp.int32)
    valid = jnp.clip((goff_g[tec] + tot_g[tec] - tstart_g) * GRAN, 0, EXP_TM)
    valid = jnp.where(te >= N_EXPERTS, 0, valid).astype(jnp.int32)
    locoff = jnp.broadcast_to((gl * GRAN).astype(F32)[:, :, None],
                              (cnt.shape[0], N_EXPERTS, MOE_TILE))
    ztail = jnp.concatenate([(goff_g + tot_g) * GRAN, region_g - tot_g,
                             jnp.sum(region_g, keepdims=True) // tm_g]).astype(jnp.int32)
    return ngtot.astype(jnp.int32), gdst, tec, valid, locoff, ztail


def _sorted_slots(rinfo_ref, locoff_ref, upper_ref):
    rt = jnp.transpose(rinfo_ref[...])
    i1, i2, w1, w2 = rt[0:1, :], rt[1:2, :], rt[2:3, :], rt[3:4, :]
    e_iota = lax.broadcasted_iota(jnp.int32, (N_EXPERTS, MOE_TILE), 0).astype(F32)
    a1 = e_iota == i1
    a2 = e_iota == i2
    a_t = jnp.where(a1, 1.0, jnp.where(a2, 1.0, 0.0)).astype(BF16)
    rank = jnp.dot(a_t, upper_ref[...], preferred_element_type=F32)
    pos = locoff_ref[0] + rank
    d1 = jnp.sum(jnp.where(a1, pos, 0.0), axis=0, keepdims=True)
    d2 = jnp.sum(jnp.where(a2, pos, 0.0), axis=0, keepdims=True)
    r_iota = lax.broadcasted_iota(jnp.int32, (SORT_ROWS, MOE_TILE), 0).astype(F32)
    return r_iota, d1, d2, w1, w2


def _granule_copy(src, dst, sem):
    return pltpu.make_async_copy(src, dst, sem)


def _dispatch_kernel(ngtot_s, gdst_s, ztail_s, h_ref, rinfo_ref, locoff_ref, upper_ref, xs_hbm,
                     buf, zbuf, sem):
    i = pl.program_id(0)
    last = pl.num_programs(0) - 1
    slot = lax.rem(i, 2)
    n_et = xs_hbm.shape[0] // EXP_TM

    def zero_tails(wait):
        def go(cp):
            cp.wait() if wait else cp.start()

        for e in range(N_EXPERTS):
            def gran(g, c, e=e):
                row = pl.multiple_of(ztail_s[e] + g * GRAN, GRAN)
                go(_granule_copy(zbuf.at[pl.ds(0, GRAN), :], xs_hbm.at[pl.ds(row, GRAN), :],
                                 sem.at[2]))
                return c
            lax.fori_loop(0, ztail_s[N_EXPERTS + e], gran, 0)

        def tile(tl, c):
            row = pl.multiple_of(tl * EXP_TM, EXP_TM)
            go(_granule_copy(zbuf, xs_hbm.at[pl.ds(row, EXP_TM), :], sem.at[2]))
            return c
        lax.fori_loop(ztail_s[2 * N_EXPERTS], n_et, tile, 0)

    @pl.when(i == 0)
    def _():
        zbuf[...] = jnp.zeros_like(zbuf)
        zero_tails(wait=False)

    def copy(tile, g, sl):
        src = buf.at[sl, pl.ds(pl.multiple_of(g * GRAN, GRAN), GRAN), :]
        row = pl.multiple_of(gdst_s[tile * (MAX_GRAN + 1) + g], GRAN)
        return _granule_copy(src, xs_hbm.at[pl.ds(row, GRAN), :], sem.at[sl])

    def wait_tile(tile, sl):
        def body(g, c):
            copy(tile, g, sl).wait()
            return c
        lax.fori_loop(0, ngtot_s[tile], body, 0)

    @pl.when(i >= 2)
    def _():
        wait_tile(i - 2, slot)

    r_iota, d1, d2, _, _ = _sorted_slots(rinfo_ref, locoff_ref, upper_ref)
    sel = jnp.where(r_iota == d1, 1.0, jnp.where(r_iota == d2, 1.0, 0.0)).astype(BF16)
    buf[slot] = jnp.dot(sel, h_ref[...], preferred_element_type=F32).astype(BF16)

    def start(g, c):
        copy(i, g, slot).start()
        return c
    lax.fori_loop(0, ngtot_s[i], start, 0)

    @pl.when(i == last)
    def _():
        @pl.when(i >= 1)
        def _():
            wait_tile(i - 1, 1 - slot)
        wait_tile(i, slot)
        zero_tails(wait=True)


def _dispatch(plan, h2, rinfo, upper):
    ngtot, gdst, _, _, locoff, ztail = plan
    t, d = h2.shape
    n_tiles = t // MOE_TILE
    grid_spec = pltpu.PrefetchScalarGridSpec(
        num_scalar_prefetch=3,
        grid=(n_tiles,),
        in_specs=[
            pl.BlockSpec((MOE_TILE, d), lambda i, *_: (i, 0)),
            pl.BlockSpec((MOE_TILE, LANES), lambda i, *_: (i, 0)),
            pl.BlockSpec((1, N_EXPERTS, MOE_TILE), lambda i, *_: (i, 0, 0)),
            pl.BlockSpec((MOE_TILE, MOE_TILE), lambda i, *_: (0, 0)),
        ],
        out_specs=pl.BlockSpec(memory_space=pl.ANY),
        scratch_shapes=[pltpu.VMEM((2, SORT_ROWS, d), BF16), pltpu.VMEM((EXP_TM, d), BF16),
                        pltpu.SemaphoreType.DMA((3,))],
    )
    return pl.pallas_call(
        _dispatch_kernel,
        grid_spec=grid_spec,
        out_shape=jax.ShapeDtypeStruct((_moe_rows(t), d), BF16),
        compiler_params=pltpu.CompilerParams(
            dimension_semantics=("arbitrary",), vmem_limit_bytes=VMEM_LIMIT),
        name="moe_dispatch",
    )(ngtot, gdst, ztail, h2, rinfo, locoff, upper)


def _expert_kernel(te_s, valid_s, x_ref, wg_ref, wu_ref, wd_ref, y_ref):
    valid = valid_s[pl.program_id(0)]

    @pl.when(valid > 0)
    def _():
        x = x_ref[...]
        a = jnp.dot(x, wg_ref[0].astype(BF16), preferred_element_type=F32)
        u = jnp.dot(x, wu_ref[0].astype(BF16), preferred_element_type=F32)
        hid = ((a * _sigmoid(a)) * u).astype(BF16)
        y_ref[...] = jnp.dot(hid, wd_ref[0].astype(BF16),
                             preferred_element_type=F32).astype(y_ref.dtype)

    @pl.when(valid <= 0)
    def _():
        y_ref[...] = jnp.zeros_like(y_ref)


def _experts(plan, xs, w_gate, w_up, w_down, layer):
    _, _, tec, valid, _, _ = plan
    rows, d = xs.shape
    _, _, _, ff = w_gate.shape
    grid_spec = pltpu.PrefetchScalarGridSpec(
        num_scalar_prefetch=2,
        grid=(rows // EXP_TM,),
        in_specs=[
            pl.BlockSpec((EXP_TM, d), lambda i, te, va: (i, 0)),
            pl.BlockSpec((None, 1, d, ff), lambda i, te, va: (layer, te[i], 0, 0)),
            pl.BlockSpec((None, 1, d, ff), lambda i, te, va: (layer, te[i], 0, 0)),
            pl.BlockSpec((None, 1, ff, d), lambda i, te, va: (layer, te[i], 0, 0)),
        ],
        out_specs=pl.BlockSpec((EXP_TM, d), lambda i, te, va: (i, 0)),
    )
    return pl.pallas_call(
        _expert_kernel,
        grid_spec=grid_spec,
        out_shape=jax.ShapeDtypeStruct((rows, d), BF16),
        compiler_params=pltpu.CompilerParams(
            dimension_semantics=("arbitrary",), vmem_limit_bytes=VMEM_LIMIT),
        name="moe_experts",
    )(tec, valid, xs, w_gate, w_up, w_down)


def _combine_kernel(ngtot_s, gdst_s, ys_hbm, rinfo_ref, locoff_ref, upper_ref, x1_ref, mod_ref,
                    lng_ref, lnb_ref, o_ref, buf, sem):
    i = pl.program_id(0)
    last = pl.num_programs(0) - 1
    slot = lax.rem(i, 2)

    def copy(tile, g, sl):
        row = pl.multiple_of(gdst_s[tile * (MAX_GRAN + 1) + g], GRAN)
        dst = buf.at[sl, pl.ds(pl.multiple_of(g * GRAN, GRAN), GRAN), :]
        return _granule_copy(ys_hbm.at[pl.ds(row, GRAN), :], dst, sem.at[sl])

    def start_tile(tile, sl):
        def body(g, c):
            copy(tile, g, sl).start()
            return c
        lax.fori_loop(0, ngtot_s[tile], body, 0)

    @pl.when(i == 0)
    def _():
        buf[...] = jnp.zeros_like(buf)
        start_tile(0, 0)

    @pl.when(i < last)
    def _():
        start_tile(i + 1, 1 - slot)

    def wait(g, c):
        copy(i, g, slot).wait()
        return c
    lax.fori_loop(0, ngtot_s[i], wait, 0)

    r_iota, d1, d2, w1, w2 = _sorted_slots(rinfo_ref, locoff_ref, upper_ref)
    wsel = jnp.where(r_iota == d1, w1, jnp.where(r_iota == d2, w2, 0.0)).astype(BF16)
    moe = lax.dot_general(wsel, buf[slot], (((0,), (0,)), ((), ())),
                          preferred_element_type=F32)
    gate2 = mod_ref[5:6, :]
    xn = _layer_norm_rows(DEEPNORM_ALPHA * x1_ref[...] + (1.0 + gate2) * moe)
    o_ref[...] = xn * lng_ref[...] + lnb_ref[...]


def _combine(plan, ys, rinfo, upper, x1, mod_l, ln_g, ln_b, seq):
    ngtot, gdst, _, _, locoff, _ = plan
    t, d = x1.shape
    tiles_per_batch = seq // MOE_TILE
    grid_spec = pltpu.PrefetchScalarGridSpec(
        num_scalar_prefetch=2,
        grid=(t // MOE_TILE,),
        in_specs=[
            pl.BlockSpec(memory_space=pl.ANY),
            pl.BlockSpec((MOE_TILE, LANES), lambda i, *_: (i, 0)),
            pl.BlockSpec((1, N_EXPERTS, MOE_TILE), lambda i, *_: (i, 0, 0)),
            pl.BlockSpec((MOE_TILE, MOE_TILE), lambda i, *_: (0, 0)),
            pl.BlockSpec((MOE_TILE, d), lambda i, *_: (i, 0)),
            pl.BlockSpec((None, N_MOD, d), lambda i, *_: (i // tiles_per_batch, 0, 0)),
            pl.BlockSpec((1, d), lambda i, *_: (0, 0)),
            pl.BlockSpec((1, d), lambda i, *_: (0, 0)),
        ],
        out_specs=pl.BlockSpec((MOE_TILE, d), lambda i, *_: (i, 0)),
        scratch_shapes=[pltpu.VMEM((2, SORT_ROWS, d), BF16), pltpu.SemaphoreType.DMA((2,))],
    )
    return pl.pallas_call(
        _combine_kernel,
        grid_spec=grid_spec,
        out_shape=jax.ShapeDtypeStruct((t, d), F32),
        compiler_params=pltpu.CompilerParams(
            dimension_semantics=("arbitrary",), vmem_limit_bytes=VMEM_LIMIT),
        name="moe_combine",
    )(ngtot, gdst, ys, rinfo, locoff, upper, x1, mod_l, ln_g, ln_b)


def _pad_lanes(a):
    return jnp.pad(a, ((0, 0), (0, LANES - a.shape[1])))


def kernel(x, c, w_mod, b_mod, w_in, conv_w, conv_b, dt_bias, a_log, d_skip, ssd_norm_w, forget_b,
           w_ssd_o, w_att_o, w_o, ln1_g, ln1_b, w_router_group, b_router_group, w_router_expert,
           b_router_expert, w_gate, w_up, w_down, ln2_g, ln2_b):
    bsz, seq, d = x.shape
    depth = w_mod.shape[0]
    t = bsz * seq
    assert seq % MOE_TILE == 0 and d == SSD_INNER
    tm_proj = min(512, seq)
    ssd_rows = min(512, seq)
    upper = jnp.asarray(np.triu(np.ones((MOE_TILE, MOE_TILE), np.float32), k=1), BF16)
    tq = min(512, seq)
    tk = min(512, seq)

    mod = _modulation(c, w_mod, b_mod).reshape(depth, bsz, N_MOD, d)
    consts = _ssd_constants()
    placement = _attn_placement()
    in_sizes = (SSD_INNER, SSD_CONV_CH, SSD_HEADS, ATT_INNER, ATT_INNER, ATT_INNER, ATT_HEADS, d, d)
    offs = np.concatenate([[0], np.cumsum(in_sizes)]).tolist()

    x2 = x.reshape(t, d)
    for l in range(depth):
        cols = [w_in[l][:, offs[i]:offs[i + 1]] for i in range(len(in_sizes))]
        wz, wxbc, wdt, wq, wk, wv, wf, wgs, wga = cols
        wq = wq * (math.log2(math.e) * ATT_HEAD_DIM ** -0.5)
        big = [w.astype(BF16) for w in (wz, wxbc, wq, wk, wv, wgs, wga)]
        w_small = _pad_lanes(jnp.concatenate([wdt, wf], axis=1))
        z, xbc, q, k, v, gs, ga, small = _inproj(x2, mod[l], big, w_small, seq, tm_proj)

        hvec = jnp.concatenate([
            _pad_lanes(jnp.concatenate([dt_bias[l], forget_b[l]])[None, :]),
            _pad_lanes(a_log[l][None, :]),
            jnp.zeros((6, LANES), F32)], axis=0)
        dskip_x = jnp.repeat(d_skip[l], SSD_HEAD_DIM)[None, :]
        y, fcum = _ssd(xbc.reshape(bsz, seq, SSD_CONV_CH), small.reshape(bsz, seq, LANES),
                       conv_w[l], conv_b[l][None, :], hvec, dskip_x, consts, ssd_rows)

        o = _attention(q.reshape(bsz, seq, ATT_INNER), k.reshape(bsz, seq, ATT_INNER),
                       v.reshape(bsz, seq, ATT_INNER), fcum, placement, tq, tk)

        w_r = _pad_lanes(jnp.concatenate([w_router_expert[l], w_router_group[l]], axis=1))
        b_r = _pad_lanes(jnp.concatenate([b_router_expert[l], b_router_group[l]])[None, :])
        x1, h2, rinfo, cnt = _mixout(
            y.reshape(t, SSD_INNER), z, o.reshape(t, ATT_INNER), gs, ga, x2, mod[l],
            ssd_norm_w[l][None, :], w_ssd_o[l].astype(BF16), w_att_o[l].astype(BF16),
            w_o[l].astype(BF16), ln1_g[l][None, :], ln1_b[l][None, :], w_r, b_r, seq, MOE_TILE)

        plan = _moe_plan(cnt[:, 0, :N_EXPERTS], t)
        xs = _dispatch(plan, h2, rinfo, upper)
        ys = _experts(plan, xs, w_gate, w_up, w_down, l)
        x2 = _combine(plan, ys, rinfo, upper, x1, mod[l], ln2_g[l][None, :], ln2_b[l][None, :],
                      seq)
    return x2.reshape(bsz, seq, d)
```

```python
import functools
import math

import numpy as np
import jax
import jax.numpy as jnp
from jax import lax
from jax.experimental import pallas as pl
from jax.experimental.pallas import tpu as pltpu

SSD_HEAD_DIM = 64
SSD_HEADS = 16
SSD_GROUPS = 4
SSD_STATE = 128
SSD_CHUNK = 128
SSD_CONV = 4
SSD_INNER = SSD_HEADS * SSD_HEAD_DIM
SSD_BC = SSD_GROUPS * SSD_STATE
SSD_CONV_CH = SSD_INNER + 2 * SSD_BC
ATT_HEAD_DIM = 64
ATT_HEADS = 8
ATT_INNER = ATT_HEADS * ATT_HEAD_DIM
N_GROUPS = 4
EXPERTS_PER_GROUP = 4
N_EXPERTS = 16
N_MOD = 6
DEPTH_FOR_DEEPNORM = 4
DEEPNORM_ALPHA = (2 * DEPTH_FOR_DEEPNORM) ** 0.25
LN_EPS = 1e-5

LANES = 128
VMEM_LIMIT = 56 * 1024 * 1024

F32 = jnp.float32
BF16 = jnp.bfloat16


def _sigmoid(x):
    return 1.0 / (1.0 + jnp.exp2(x * (-math.log2(math.e))))


def _layer_norm_rows(x):
    mu = jnp.mean(x, axis=-1, keepdims=True)
    xc = x - mu
    var = jnp.mean(xc * xc, axis=-1, keepdims=True)
    return xc * lax.rsqrt(var + LN_EPS)


def _dot_split(h, h_hi, w):
    h_lo = (h - h_hi.astype(F32)).astype(BF16)
    w_hi = w.astype(BF16)
    w_lo = (w - w_hi.astype(F32)).astype(BF16)
    r = jnp.dot(h_hi, jnp.concatenate([w_hi, w_lo], axis=1), preferred_element_type=F32)
    return r[:, :LANES] + r[:, LANES:] + jnp.dot(h_lo, w_hi, preferred_element_type=F32)


def _const_spec(shape):
    n = len(shape)
    return pl.BlockSpec(shape, lambda *_: (0,) * n)


def _mod_kernel(c_ref, w_ref, b_ref, o_ref):
    c = c_ref[...]
    ca = c * _sigmoid(c)
    ca_hi = ca.astype(BF16)
    ca_lo = (ca - ca_hi.astype(F32)).astype(BF16)
    w = w_ref[0]
    w_hi = w.astype(BF16)
    w_lo = (w - w_hi.astype(F32)).astype(BF16)
    o_ref[0] = (jnp.dot(ca_hi, w_hi, preferred_element_type=F32)
                + jnp.dot(ca_hi, w_lo, preferred_element_type=F32)
                + jnp.dot(ca_lo, w_hi, preferred_element_type=F32) + b_ref[0])


def _modulation(c, w_mod, b_mod):
    depth, d, n = w_mod.shape
    bsz = c.shape[0]
    tn = 1024
    return pl.pallas_call(
        _mod_kernel,
        grid=(depth, n // tn),
        in_specs=[
            pl.BlockSpec((bsz, d), lambda l, j: (0, 0)),
            pl.BlockSpec((1, d, tn), lambda l, j: (l, 0, j)),
            pl.BlockSpec((1, 1, tn), lambda l, j: (l, 0, j)),
        ],
        out_specs=pl.BlockSpec((1, bsz, tn), lambda l, j: (l, 0, j)),
        out_shape=jax.ShapeDtypeStruct((depth, bsz, n), F32),
        compiler_params=pltpu.CompilerParams(
            dimension_semantics=("arbitrary", "arbitrary"), vmem_limit_bytes=VMEM_LIMIT),
        name="modulation",
    )(c, w_mod, b_mod.reshape(depth, 1, n))


def _inproj_kernel(x_ref, mod_ref, wz_ref, wxbc_ref, wq_ref, wk_ref, wv_ref, wgs_ref, wga_ref,
                   wsm_ref, z_ref, xbc_ref, q_ref, k_ref, v_ref, gs_ref, ga_ref, sm_ref):
    x = x_ref[...]
    shift = mod_ref[0:1, :]
    scale = mod_ref[1:2, :]
    h = _layer_norm_rows(x) * (1.0 + scale) + shift
    hb = h.astype(BF16)
    for w_ref, o_ref in ((wz_ref, z_ref), (wxbc_ref, xbc_ref), (wq_ref, q_ref), (wk_ref, k_ref),
                         (wv_ref, v_ref), (wgs_ref, gs_ref), (wga_ref, ga_ref)):
        o_ref[...] = jnp.dot(hb, w_ref[...], preferred_element_type=F32).astype(o_ref.dtype)
    sm_ref[...] = _dot_split(h, hb, wsm_ref[...])


def _inproj(x2, mod_l, weights, w_small, seq, tm):
    t, d = x2.shape
    tiles_per_batch = seq // tm
    widths = [w.shape[1] for w in weights]
    in_specs = [
        pl.BlockSpec((tm, d), lambda i: (i, 0)),
        pl.BlockSpec((None, N_MOD, d), lambda i: (i // tiles_per_batch, 0, 0)),
    ]
    in_specs += [_const_spec(w.shape) for w in weights]
    in_specs += [_const_spec(w_small.shape)]
    out_specs = [pl.BlockSpec((tm, n), lambda i: (i, 0)) for n in widths]
    out_specs += [pl.BlockSpec((tm, LANES), lambda i: (i, 0))]
    out_shape = [jax.ShapeDtypeStruct((t, n), BF16) for n in widths]
    out_shape += [jax.ShapeDtypeStruct((t, LANES), F32)]
    return pl.pallas_call(
        _inproj_kernel,
        grid=(t // tm,),
        in_specs=in_specs,
        out_specs=out_specs,
        out_shape=out_shape,
        compiler_params=pltpu.CompilerParams(
            dimension_semantics=("arbitrary",), vmem_limit_bytes=VMEM_LIMIT),
        name="inproj",
    )(x2, mod_l, *weights, w_small)


CONV_HIST = 16
CONV_WIN = CONV_HIST + SSD_CHUNK


def _ssd_constants():
    L = SSD_CHUNK
    sh = np.zeros((L, SSD_CONV * CONV_WIN), np.float32)
    for j in range(SSD_CONV):
        for t in range(L):
            sh[t, j * CONV_WIN + CONV_HIST + t - j] = 1.0
    tril = np.tril(np.ones((L, L), np.float32))
    e3 = np.zeros((LANES, SSD_INNER), np.float32)
    for piece in range(3):
        for h in range(SSD_HEADS):
            e3[32 * piece + h, h * SSD_HEAD_DIM:(h + 1) * SSD_HEAD_DIM] = 1.0
    r = SSD_HEADS // SSD_GROUPS
    bd = np.zeros((r * L, r * SSD_HEAD_DIM), np.float32)
    for i in range(r):
        bd[i * L:(i + 1) * L, i * SSD_HEAD_DIM:(i + 1) * SSD_HEAD_DIM] = 1.0
    return (jnp.asarray(sh, BF16), jnp.asarray(tril, BF16), jnp.asarray(e3, BF16),
            jnp.asarray(bd, BF16))


def _split3_bf16(a):
    hi = a.astype(BF16)
    r1 = a - hi.astype(F32)
    mid = r1.astype(BF16)
    lo = (r1 - mid.astype(F32)).astype(BF16)
    return hi, mid, lo


def _expand_heads(xm, e3):
    hi, mid, lo = _split3_bf16(xm)
    packed = (hi.astype(F32) + pltpu.roll(mid.astype(F32), 32, 1)
              + pltpu.roll(lo.astype(F32), 64, 1)).astype(BF16)
    return jnp.dot(packed, e3, preferred_element_type=F32)


def _ssd_kernel(xbc_ref, sm_ref, cw_ref, cb_ref, hv_ref, dskip_ref, sh_ref, tril_ref, e3_ref,
                bd_ref, y_ref, fcum_ref, uext, state, fcarry, *, rows):
    L = SSD_CHUNK
    j = pl.program_id(1)

    @pl.when(j == 0)
    def _():
        uext[0:CONV_HIST, :] = jnp.zeros((CONV_HIST, SSD_CONV_CH), BF16)
        state[...] = jnp.zeros_like(state)
        fcarry[...] = jnp.zeros_like(fcarry)

    uext[CONV_HIST:CONV_HIST + rows, :] = xbc_ref[0]
    cw_b = cw_ref[...].astype(BF16)

    lane = lax.broadcasted_iota(jnp.int32, (1, LANES), 1)
    is_dt = lane < SSD_HEADS
    is_f = jnp.logical_and(lane >= SSD_HEADS, lane < SSD_HEADS + ATT_HEADS)
    a_neg = -jnp.exp(hv_ref[1:2, :])
    bias = hv_ref[0:1, :]
    rr = lax.broadcasted_iota(jnp.int32, (L, L), 0)
    cc = lax.broadcasted_iota(jnp.int32, (L, L), 1)
    causal = rr >= cc
    gw = (SSD_HEADS // SSD_GROUPS) * SSD_HEAD_DIM

    for c in range(rows // L):
        r0 = c * L
        def conv_act(lo, hi):
            win = uext[r0:r0 + CONV_WIN, lo:hi]
            taps = jnp.concatenate([win * cw_b[SSD_CONV - 1 - j:SSD_CONV - j, lo:hi]
                                    for j in range(SSD_CONV)], axis=0)
            conv = jnp.dot(sh_ref[...], taps, preferred_element_type=F32) + cb_ref[:, lo:hi]
            return conv * _sigmoid(conv)

        xs = conv_act(0, SSD_INNER)
        bm = conv_act(SSD_INNER, SSD_INNER + SSD_BC)
        cm = conv_act(SSD_INNER + SSD_BC, SSD_CONV_CH)

        pre = sm_ref[0, r0:r0 + L, :] + bias
        tail = jnp.log(1.0 + jnp.exp(-jnp.abs(pre)))
        dt = jnp.where(is_dt, jnp.maximum(pre, 0.0) + tail, 0.0)
        log_f = jnp.minimum(pre, 0.0) - tail
        comb = jnp.where(is_dt, dt * a_neg, jnp.where(is_f, log_f, 0.0))
        c3 = jnp.dot(tril_ref[...], jnp.concatenate(_split3_bf16(comb), axis=1),
                     preferred_element_type=F32)
        cs = c3[:, :LANES] + c3[:, LANES:2 * LANES] + c3[:, 2 * LANES:]
        fc = cs + fcarry[...]
        fcarry[...] = jnp.where(is_f, fc[L - 1:L, :], 0.0)
        f_hi, f_mid, f_lo = _split3_bf16(jnp.where(is_f, fc * math.log2(math.e), 0.0))
        fcum_ref[0, r0:r0 + L, :] = (
            f_hi.astype(F32) + pltpu.roll(f_mid.astype(F32), 32, 1)
            + pltpu.roll(f_lo.astype(F32), 64, 1)
            + jnp.where(lane == LANES - 1, 1.0, 0.0)).astype(BF16)

        a_cs = jnp.where(is_dt, cs, 0.0)
        a_cs_x = _expand_heads(a_cs, e3_ref[...])
        dt_x = _expand_heads(dt, e3_ref[...])
        a_last_x = a_cs_x[L - 1:L, :]
        xdt = xs * dt_x
        xdt_b = xdt.astype(BF16)
        xend_b = (xdt * jnp.exp(a_last_x - a_cs_x)).astype(BF16)
        e_acs = jnp.exp(a_cs_x)
        e_last = jnp.exp(a_last_x)
        a_cs_t = jnp.transpose(a_cs)

        y_groups = []
        for g in range(SSD_GROUPS):
            bg = bm[:, g * SSD_STATE:(g + 1) * SSD_STATE]
            cg_b = cm[:, g * SSD_STATE:(g + 1) * SSD_STATE].astype(BF16)
            cb = lax.dot_general(cg_b, bg.astype(BF16), (((1,), (1,)), ((), ())),
                                 preferred_element_type=F32)
            ms = []
            for r in range(SSD_HEADS // SSD_GROUPS):
                h = g * (SSD_HEADS // SSD_GROUPS) + r
                seg = a_cs[:, h:h + 1] - a_cs_t[h:h + 1, :]
                dec = jnp.exp(jnp.where(causal, seg, -jnp.inf))
                ms.append((cb * dec).astype(BF16))
            m_cat = jnp.concatenate(ms, axis=1)
            xg = xdt_b[:, g * gw:(g + 1) * gw]
            x_bd = jnp.concatenate([xg] * (SSD_HEADS // SSD_GROUPS), axis=0) * bd_ref[...]
            y_diag = jnp.dot(m_cat, x_bd, preferred_element_type=F32)
            st = state[:, g * gw:(g + 1) * gw]
            y_off = jnp.dot(cg_b, st.astype(BF16), preferred_element_type=F32)
            y_groups.append(y_diag + y_off * e_acs[:, g * gw:(g + 1) * gw])
            new_st = jnp.dot(jnp.transpose(bg).astype(BF16), xend_b[:, g * gw:(g + 1) * gw],
                             preferred_element_type=F32)
            state[:, g * gw:(g + 1) * gw] = st * e_last[:, g * gw:(g + 1) * gw] + new_st
        y = jnp.concatenate(y_groups, axis=1) + dskip_ref[...] * xs
        y_ref[0, r0:r0 + L, :] = y.astype(y_ref.dtype)

    uext[0:CONV_HIST, :] = uext[rows:rows + CONV_HIST, :]


def _ssd(xbc, small, conv_w, conv_b, hvec, dskip_x, consts, rows):
    bsz, seq, _ = xbc.shape
    sh, tril, e3, bd = consts
    kern = functools.partial(_ssd_kernel, rows=rows)
    return pl.pallas_call(
        kern,
        grid=(bsz, seq // rows),
        in_specs=[
            pl.BlockSpec((1, rows, SSD_CONV_CH), lambda b, j: (b, j, 0)),
            pl.BlockSpec((1, rows, LANES), lambda b, j: (b, j, 0)),
            _const_spec(conv_w.shape), _const_spec(conv_b.shape), _const_spec(hvec.shape),
            _const_spec(dskip_x.shape), _const_spec(sh.shape), _const_spec(tril.shape),
            _const_spec(e3.shape), _const_spec(bd.shape),
        ],
        out_specs=[
            pl.BlockSpec((1, rows, SSD_INNER), lambda b, j: (b, j, 0)),
            pl.BlockSpec((1, rows, LANES), lambda b, j: (b, j, 0)),
        ],
        out_shape=[
            jax.ShapeDtypeStruct((bsz, seq, SSD_INNER), BF16),
            jax.ShapeDtypeStruct((bsz, seq, LANES), BF16),
        ],
        scratch_shapes=[
            pltpu.VMEM((CONV_HIST + rows, SSD_CONV_CH), BF16),
            pltpu.VMEM((SSD_STATE, SSD_INNER), F32),
            pltpu.VMEM((1, LANES), F32),
        ],
        compiler_params=pltpu.CompilerParams(
            dimension_semantics=("arbitrary", "arbitrary"), vmem_limit_bytes=VMEM_LIMIT),
        name="ssd",
    )(xbc, small, conv_w, conv_b, hvec, dskip_x, sh, tril, e3, bd)


def _attn_placement():
    pq = np.zeros((ATT_HEADS // 2, LANES, 2 * LANES), np.float32)
    pk = np.zeros_like(pq)
    for p in range(ATT_HEADS // 2):
        for hd in range(2):
            h = 2 * p + hd
            col = hd * LANES + (ATT_HEAD_DIM if hd == 0 else 0)
            for piece in range(3):
                pq[p, 32 * piece + SSD_HEADS + h, col + piece] = 1.0
                pq[p, LANES - 1, col + 3 + piece] = 1.0
                pk[p, LANES - 1, col + piece] = 1.0
                pk[p, 32 * piece + SSD_HEADS + h, col + 3 + piece] = -1.0
    return jnp.asarray(pq, BF16), jnp.asarray(pk, BF16)


def _attn_kernel(q_ref, k_ref, v_ref, fq_ref, fk_ref, pq_ref, pk_ref, o_ref, kaug_sc, s_sc,
                 m_sc, acc_sc, *, tq, tk):
    qi = pl.program_id(2)
    seq = k_ref.shape[1]
    r = tq // tk
    n_full = qi * r
    key_idx = lax.broadcasted_iota(jnp.int32, (tk, tq), 0)
    qry_idx = lax.broadcasted_iota(jnp.int32, (tk, tq), 1)
    lane2 = lax.broadcasted_iota(jnp.int32, (1, 2 * LANES), 1)
    keep = jnp.logical_or(lane2 < ATT_HEAD_DIM, lane2 >= LANES + ATT_HEAD_DIM)

    def augment(x, f, place):
        aug = jnp.dot(f, place, preferred_element_type=F32).astype(BF16)
        return jnp.where(keep, jnp.concatenate([x, x], axis=1), aug)

    npair = q_ref.shape[2] // LANES
    heads = range(2 * npair)
    pair_cols = lambda pp: slice(pp * LANES, (pp + 1) * LANES)
    slab_cols = lambda h: slice(h * LANES, (h + 1) * LANES)

    @pl.when(qi == 0)
    def _():
        def fill(c, carry):
            rows = pl.ds(pl.multiple_of(c * tk, tk), tk)
            for pp in range(npair):
                kaug_sc[rows, pp * 2 * LANES:(pp + 1) * 2 * LANES] = augment(
                    k_ref[0, rows, pair_cols(pp)], fk_ref[0, rows, :], pk_ref[pp])
            return carry
        lax.fori_loop(0, seq // tk, fill, 0)

    q_aug_t = []
    for pp in range(npair):
        qa = augment(q_ref[0, :, pair_cols(pp)], fq_ref[0], pq_ref[pp]).astype(F32)
        q_aug_t += [jnp.transpose(qa[:, slab_cols(hd)]).astype(BF16) for hd in range(2)]

    def scores(kblk):
        start = pl.multiple_of(kblk * tk, tk)
        kb = kaug_sc[pl.ds(start, tk), :]
        return [jnp.dot(kb[:, slab_cols(h)], q_aug_t[h], preferred_element_type=F32)
                for h in heads]

    def prepare(s_list, diag):
        mcols = []
        for h in heads:
            s_t = s_list[h]
            if diag is not None:
                s_t = jnp.where(key_idx + diag * tk <= qry_idx, s_t, -jnp.inf)
            s_sc[h] = s_t
            mcols.append(jnp.max(s_t, axis=0, keepdims=True))
        return mcols

    def consume(kblk, mcols):
        start = pl.multiple_of(kblk * tk, tk)
        vb = v_ref[0, pl.ds(start, tk), :]
        for h in heads:
            m_old = m_sc[h]
            m_new = jnp.maximum(m_old, mcols[h])
            p_t = jnp.exp2((s_sc[h] - m_new).astype(BF16))
            alpha = jnp.exp2(m_old - m_new)
            vb_p = vb[:, pair_cols(h // 2)]
            vb_h = jnp.where(own_lanes[h % 2], vb_p, jnp.ones_like(vb_p))
            pv = lax.dot_general(vb_h, p_t, (((0,), (0,)), ((), ())),
                                 preferred_element_type=F32)
            acc_sc[h] = alpha * acc_sc[h] + pv
            m_sc[h] = m_new

    lane1 = lax.broadcasted_iota(jnp.int32, (1, LANES), 1)
    own_lanes = (lane1 < ATT_HEAD_DIM, lane1 >= ATT_HEAD_DIM)
    m_sc[...] = jnp.full_like(m_sc, -jnp.inf)
    acc_sc[...] = jnp.zeros_like(acc_sc)
    mcols = prepare(scores(n_full), 0)
    for d in range(r):
        s_next = scores(n_full + d + 1 if d + 1 < r else 0)
        consume(n_full + d, mcols)
        mcols = prepare(s_next, d + 1 if d + 1 < r else None)

    def body(jb, mcols):
        s_next = scores(jnp.minimum(jb + 1, n_full - 1))
        consume(jb, mcols)
        return tuple(prepare(s_next, None))

    lax.fori_loop(0, n_full, body, tuple(mcols))
    hdim = ATT_HEAD_DIM
    for pp in range(npair):
        acc0, acc1 = acc_sc[2 * pp], acc_sc[2 * pp + 1]
        o_t = jnp.concatenate([acc0[:hdim] / acc0[hdim:hdim + 1], acc1[hdim:] / acc1[0:1]],
                              axis=0)
        o_ref[0, :, pair_cols(pp)] = jnp.transpose(o_t).astype(o_ref.dtype)


ATT_PAIRS_PER_STEP = 4


def _attention(q, k, v, f_pieces, placement, tq, tk):
    bsz, seq, _ = q.shape
    npair = ATT_PAIRS_PER_STEP
    steps = ATT_HEADS // (2 * npair)
    pq, pk = placement
    kern = functools.partial(_attn_kernel, tq=tq, tk=tk)
    wide = npair * LANES
    return pl.pallas_call(
        kern,
        grid=(bsz, steps, seq // tq),
        in_specs=[
            pl.BlockSpec((1, tq, wide), lambda b, p, i: (b, i, p)),
            pl.BlockSpec((1, seq, wide), lambda b, p, i: (b, 0, p)),
            pl.BlockSpec((1, seq, wide), lambda b, p, i: (b, 0, p)),
            pl.BlockSpec((1, tq, LANES), lambda b, p, i: (b, i, 0)),
            pl.BlockSpec((1, seq, LANES), lambda b, p, i: (b, 0, 0)),
            pl.BlockSpec((npair, LANES, 2 * LANES), lambda b, p, i: (p, 0, 0)),
            pl.BlockSpec((npair, LANES, 2 * LANES), lambda b, p, i: (p, 0, 0)),
        ],
        out_specs=pl.BlockSpec((1, tq, wide), lambda b, p, i: (b, i, p)),
        out_shape=jax.ShapeDtypeStruct((bsz, seq, ATT_INNER), BF16),
        scratch_shapes=[pltpu.VMEM((seq, 2 * wide), BF16),
                        pltpu.VMEM((2 * npair, tk, tq), F32),
                        pltpu.VMEM((2 * npair, 1, tq), F32),
                        pltpu.VMEM((2 * npair, LANES, tq), F32)],
        compiler_params=pltpu.CompilerParams(
            dimension_semantics=("arbitrary", "arbitrary", "arbitrary"),
            vmem_limit_bytes=VMEM_LIMIT),
        name="fox_attention",
    )(q, k, v, f_pieces, f_pieces, pq, pk)


def _route(logits):
    lane = lax.broadcasted_iota(jnp.int32, logits.shape, 1)
    neg = -jnp.inf
    big = jnp.int32(1 << 20)
    is_g = jnp.logical_and(lane >= N_EXPERTS, lane < N_EXPERTS + N_GROUPS)
    gl = jnp.where(is_g, logits, neg)
    gmax = jnp.max(gl, axis=1, keepdims=True)
    gsum = jnp.sum(jnp.exp(gl - gmax), axis=1, keepdims=True)
    g_p = 1.0 / gsum
    g_idx = jnp.min(jnp.where(gl == gmax, lane, big), axis=1, keepdims=True) - N_EXPERTS
    lo = g_idx * EXPERTS_PER_GROUP
    in_group = jnp.logical_and(lane >= lo, lane < lo + EXPERTS_PER_GROUP)
    el = jnp.where(in_group, logits, neg)
    m1 = jnp.max(el, axis=1, keepdims=True)
    i1 = jnp.min(jnp.where(el == m1, lane, big), axis=1, keepdims=True)
    el2 = jnp.where(lane == i1, neg, el)
    m2 = jnp.max(el2, axis=1, keepdims=True)
    i2 = jnp.min(jnp.where(el2 == m2, lane, big), axis=1, keepdims=True)
    e2 = jnp.exp(m2 - m1)
    w1 = g_p / (1.0 + e2)
    w2 = g_p * e2 / (1.0 + e2)
    info = jnp.where(lane == 0, i1.astype(F32), jnp.where(lane == 1, i2.astype(F32),
                     jnp.where(lane == 2, w1, jnp.where(lane == 3, w2, 0.0))))
    chosen = jnp.where(lane == i1, 1.0, jnp.where(lane == i2, 1.0, 0.0))
    return info, jnp.sum(chosen, axis=0, keepdims=True)


def _mixout_kernel(y_ref, z_ref, o_ref, gs_ref, ga_ref, x_ref, mod_ref, nw_ref, wso_ref, wao_ref,
                   wo_ref, lng_ref, lnb_ref, wr_ref, br_ref, x1_ref, h2_ref, rinfo_ref, cnt_ref):
    tm = x_ref.shape[0]
    parts = 2
    rows_per = tm // parts
    counts = jnp.zeros((1, LANES), F32)
    for part in range(parts):
        rs = slice(part * rows_per, (part + 1) * rows_per)
        g = y_ref[rs, :].astype(F32)
        z = z_ref[rs, :].astype(F32)
        g = g * (z * _sigmoid(z))
        g = g * lax.rsqrt(jnp.mean(g * g, axis=-1, keepdims=True) + LN_EPS) * nw_ref[...]
        y_ssd = jnp.dot(g.astype(BF16), wso_ref[...], preferred_element_type=F32)
        y_att = jnp.dot(o_ref[rs, :], wao_ref[...], preferred_element_type=F32)
        merged = (_sigmoid(gs_ref[rs, :].astype(F32)) * y_ssd
                  + _sigmoid(ga_ref[rs, :].astype(F32)) * y_att)
        mix = jnp.dot(merged.astype(BF16), wo_ref[...], preferred_element_type=F32)
        gate1 = mod_ref[2:3, :]
        x1 = _layer_norm_rows(DEEPNORM_ALPHA * x_ref[rs, :] + (1.0 + gate1) * mix)
        x1 = x1 * lng_ref[...] + lnb_ref[...]
        x1_ref[rs, :] = x1
        h2 = _layer_norm_rows(x1) * (1.0 + mod_ref[4:5, :]) + mod_ref[3:4, :]
        h2b = h2.astype(BF16)
        h2_ref[rs, :] = h2b
        info, cnt = _route(_dot_split(h2, h2b, wr_ref[...]) + br_ref[...])
        rinfo_ref[rs, :] = info
        counts = counts + cnt
    cnt_ref[0] = jnp.broadcast_to(counts, (8, LANES))


def _mixout(y, z, o, gs, ga, x2, mod_l, norm_w, w_ssd_o, w_att_o, w_o, ln_g, ln_b, w_r, b_r,
            seq, tm):
    t, d = x2.shape
    tiles_per_batch = seq // tm
    row = lambda n: pl.BlockSpec((tm, n), lambda i: (i, 0))
    return pl.pallas_call(
        _mixout_kernel,
        grid=(t // tm,),
        in_specs=[
            row(SSD_INNER), row(SSD_INNER), row(ATT_INNER), row(d), row(d), row(d),
            pl.BlockSpec((None, N_MOD, d), lambda i: (i // tiles_per_batch, 0, 0)),
            _const_spec(norm_w.shape), _const_spec(w_ssd_o.shape), _const_spec(w_att_o.shape),
            _const_spec(w_o.shape), _const_spec(ln_g.shape), _const_spec(ln_b.shape),
            _const_spec(w_r.shape), _const_spec(b_r.shape),
        ],
        out_specs=[row(d), row(d), row(LANES), pl.BlockSpec((1, 8, LANES), lambda i: (i, 0, 0))],
        out_shape=[
            jax.ShapeDtypeStruct((t, d), F32),
            jax.ShapeDtypeStruct((t, d), BF16),
            jax.ShapeDtypeStruct((t, LANES), F32),
            jax.ShapeDtypeStruct((t // tm, 8, LANES), F32),
        ],
        compiler_params=pltpu.CompilerParams(
            dimension_semantics=("arbitrary",), vmem_limit_bytes=VMEM_LIMIT),
        name="mixer_out",
    )(y, z, o, gs, ga, x2, mod_l, norm_w, w_ssd_o, w_att_o, w_o, ln_g, ln_b, w_r, b_r)


MOE_TILE = 512
GRAN = 16
MAX_GRAN = 2 * MOE_TILE // GRAN + N_EXPERTS - 1
SORT_ROWS = (MAX_GRAN + 1) * GRAN
EXP_TM = 512


def _moe_rows(t):
    rows = 2 * t + (t // MOE_TILE) * N_EXPERTS * (GRAN - 1) + N_EXPERTS * EXP_TM
    return -(-rows // EXP_TM) * EXP_TM


def _moe_plan(cnt, t):
    cnt = cnt.astype(jnp.int32)
    ng = (cnt + GRAN - 1) // GRAN
    gl = jnp.cumsum(ng, axis=1) - ng
    ngtot = jnp.sum(ng, axis=1)
    tot_g = jnp.sum(ng, axis=0)
    tm_g = EXP_TM // GRAN
    region_g = -(-tot_g // tm_g) * tm_g
    goff_g = jnp.cumsum(region_g) - region_g
    run_g = goff_g[None, :] + jnp.cumsum(ng, axis=0) - ng
    g_idx = jnp.arange(MAX_GRAN + 1, dtype=jnp.int32)
    owner = jnp.sum(g_idx[None, :, None] >= (gl + ng)[:, None, :], axis=2)
    owner = jnp.minimum(owner, N_EXPERTS - 1).astype(jnp.int32)
    is_owner = owner[:, :, None] == jnp.arange(N_EXPERTS, dtype=jnp.int32)[None, None, :]
    dst_g = jnp.sum(jnp.where(is_owner, (run_g - gl)[:, None, :], 0), axis=2) + g_idx[None, :]
    gdst = (dst_g * GRAN).astype(jnp.int32).reshape(-1)
    n_et = _moe_rows(t) // EXP_TM
    tstart_g = jnp.arange(n_et, dtype=jnp.int32) * tm_g
    te = jnp.sum(tstart_g[:, None] >= (goff_g + region_g)[None, :], axis=1)
    tec = jnp.minimum(te, N_EXPERTS - 1).astype(jnp.int32)
    valid = jnp.clip((goff_g[tec] + tot_g[tec] - tstart_g) * GRAN, 0, EXP_TM)
    valid = jnp.where(te >= N_EXPERTS, 0, valid).astype(jnp.int32)
    locoff = jnp.broadcast_to((gl * GRAN).astype(F32)[:, :, None],
                              (cnt.shape[0], N_EXPERTS, MOE_TILE))
    ztail = jnp.concatenate([(goff_g + tot_g) * GRAN, region_g - tot_g,
                             jnp.sum(region_g, keepdims=True) // tm_g]).astype(jnp.int32)
    return ngtot.astype(jnp.int32), gdst, tec, valid, locoff, ztail


def _sorted_slots(rinfo_ref, locoff_ref, upper_ref):
    rt = jnp.transpose(rinfo_ref[...])
    i1, i2, w1, w2 = rt[0:1, :], rt[1:2, :], rt[2:3, :], rt[3:4, :]
    e_iota = lax.broadcasted_iota(jnp.int32, (N_EXPERTS, MOE_TILE), 0).astype(F32)
    a1 = e_iota == i1
    a2 = e_iota == i2
    a_t = jnp.where(a1, 1.0, jnp.where(a2, 1.0, 0.0)).astype(BF16)
    rank = jnp.dot(a_t, upper_ref[...], preferred_element_type=F32)
    pos = locoff_ref[0] + rank
    d1 = jnp.sum(jnp.where(a1, pos, 0.0), axis=0, keepdims=True)
    d2 = jnp.sum(jnp.where(a2, pos, 0.0), axis=0, keepdims=True)
    r_iota = lax.broadcasted_iota(jnp.int32, (SORT_ROWS, MOE_TILE), 0).astype(F32)
    return r_iota, d1, d2, w1, w2


def _granule_copy(src, dst, sem):
    return pltpu.make_async_copy(src, dst, sem)


def _dispatch_kernel(ngtot_s, gdst_s, ztail_s, h_ref, rinfo_ref, locoff_ref, upper_ref, xs_hbm,
                     buf, zbuf, sem):
    i = pl.program_id(0)
    last = pl.num_programs(0) - 1
    slot = lax.rem(i, 2)
    n_et = xs_hbm.shape[0] // EXP_TM

    def zero_tails(wait):
        def go(cp):
            cp.wait() if wait else cp.start()

        for e in range(N_EXPERTS):
            def gran(g, c, e=e):
                row = pl.multiple_of(ztail_s[e] + g * GRAN, GRAN)
                go(_granule_copy(zbuf.at[pl.ds(0, GRAN), :], xs_hbm.at[pl.ds(row, GRAN), :],
                                 sem.at[2]))
                return c
            lax.fori_loop(0, ztail_s[N_EXPERTS + e], gran, 0)

        def tile(tl, c):
            row = pl.multiple_of(tl * EXP_TM, EXP_TM)
            go(_granule_copy(zbuf, xs_hbm.at[pl.ds(row, EXP_TM), :], sem.at[2]))
            return c
        lax.fori_loop(ztail_s[2 * N_EXPERTS], n_et, tile, 0)

    @pl.when(i == 0)
    def _():
        zbuf[...] = jnp.zeros_like(zbuf)
        zero_tails(wait=False)

    def copy(tile, g, sl):
        src = buf.at[sl, pl.ds(pl.multiple_of(g * GRAN, GRAN), GRAN), :]
        row = pl.multiple_of(gdst_s[tile * (MAX_GRAN + 1) + g], GRAN)
        return _granule_copy(src, xs_hbm.at[pl.ds(row, GRAN), :], sem.at[sl])

    def wait_tile(tile, sl):
        def body(g, c):
            copy(tile, g, sl).wait()
            return c
        lax.fori_loop(0, ngtot_s[tile], body, 0)

    @pl.when(i >= 2)
    def _():
        wait_tile(i - 2, slot)

    r_iota, d1, d2, _, _ = _sorted_slots(rinfo_ref, locoff_ref, upper_ref)
    sel = jnp.where(r_iota == d1, 1.0, jnp.where(r_iota == d2, 1.0, 0.0)).astype(BF16)
    buf[slot] = jnp.dot(sel, h_ref[...], preferred_element_type=F32).astype(BF16)

    def start(g, c):
        copy(i, g, slot).start()
        return c
    lax.fori_loop(0, ngtot_s[i], start, 0)

    @pl.when(i == last)
    def _():
        @pl.when(i >= 1)
        def _():
            wait_tile(i - 1, 1 - slot)
        wait_tile(i, slot)
        zero_tails(wait=True)


def _dispatch(plan, h2, rinfo, upper):
    ngtot, gdst, _, _, locoff, ztail = plan
    t, d = h2.shape
    n_tiles = t // MOE_TILE
    grid_spec = pltpu.PrefetchScalarGridSpec(
        num_scalar_prefetch=3,
        grid=(n_tiles,),
        in_specs=[
            pl.BlockSpec((MOE_TILE, d), lambda i, *_: (i, 0)),
            pl.BlockSpec((MOE_TILE, LANES), lambda i, *_: (i, 0)),
            pl.BlockSpec((1, N_EXPERTS, MOE_TILE), lambda i, *_: (i, 0, 0)),
            pl.BlockSpec((MOE_TILE, MOE_TILE), lambda i, *_: (0, 0)),
        ],
        out_specs=pl.BlockSpec(memory_space=pl.ANY),
        scratch_shapes=[pltpu.VMEM((2, SORT_ROWS, d), BF16), pltpu.VMEM((EXP_TM, d), BF16),
                        pltpu.SemaphoreType.DMA((3,))],
    )
    return pl.pallas_call(
        _dispatch_kernel,
        grid_spec=grid_spec,
        out_shape=jax.ShapeDtypeStruct((_moe_rows(t), d), BF16),
        compiler_params=pltpu.CompilerParams(
            dimension_semantics=("arbitrary",), vmem_limit_bytes=VMEM_LIMIT),
        name="moe_dispatch",
    )(ngtot, gdst, ztail, h2, rinfo, locoff, upper)


def _expert_kernel(te_s, valid_s, x_ref, wg_ref, wu_ref, wd_ref, y_ref):
    valid = valid_s[pl.program_id(0)]

    half = EXP_TM // 2

    def mlp(rows):
        x = x_ref[rows, :]
        a = jnp.dot(x, wg_ref[0].astype(BF16), preferred_element_type=F32)
        u = jnp.dot(x, wu_ref[0].astype(BF16), preferred_element_type=F32)
        hid = ((a * _sigmoid(a)) * u).astype(BF16)
        y_ref[rows, :] = jnp.dot(hid, wd_ref[0].astype(BF16),
                                 preferred_element_type=F32).astype(y_ref.dtype)

    @pl.when(valid > half)
    def _():
        mlp(slice(0, EXP_TM))

    @pl.when(jnp.logical_and(valid > 0, valid <= half))
    def _():
        mlp(slice(0, half))
        y_ref[half:, :] = jnp.zeros((EXP_TM - half, y_ref.shape[1]), y_ref.dtype)

    @pl.when(valid <= 0)
    def _():
        y_ref[...] = jnp.zeros_like(y_ref)


def _experts(plan, xs, w_gate, w_up, w_down, layer):
    _, _, tec, valid, _, _ = plan
    rows, d = xs.shape
    _, _, _, ff = w_gate.shape
    grid_spec = pltpu.PrefetchScalarGridSpec(
        num_scalar_prefetch=2,
        grid=(rows // EXP_TM,),
        in_specs=[
            pl.BlockSpec((EXP_TM, d), lambda i, te, va: (i, 0)),
            pl.BlockSpec((None, 1, d, ff), lambda i, te, va: (layer, te[i], 0, 0)),
            pl.BlockSpec((None, 1, d, ff), lambda i, te, va: (layer, te[i], 0, 0)),
            pl.BlockSpec((None, 1, ff, d), lambda i, te, va: (layer, te[i], 0, 0)),
        ],
        out_specs=pl.BlockSpec((EXP_TM, d), lambda i, te, va: (i, 0)),
    )
    return pl.pallas_call(
        _expert_kernel,
        grid_spec=grid_spec,
        out_shape=jax.ShapeDtypeStruct((rows, d), BF16),
        compiler_params=pltpu.CompilerParams(
            dimension_semantics=("arbitrary",), vmem_limit_bytes=VMEM_LIMIT),
        name="moe_experts",
    )(tec, valid, xs, w_gate, w_up, w_down)


def _combine_kernel(ngtot_s, gdst_s, ys_hbm, rinfo_ref, locoff_ref, upper_ref, x1_ref, mod_ref,
                    lng_ref, lnb_ref, o_ref, buf, sem):
    i = pl.program_id(0)
    last = pl.num_programs(0) - 1
    slot = lax.rem(i, 2)

    def copy(tile, g, sl):
        row = pl.multiple_of(gdst_s[tile * (MAX_GRAN + 1) + g], GRAN)
        dst = buf.at[sl, pl.ds(pl.multiple_of(g * GRAN, GRAN), GRAN), :]
        return _granule_copy(ys_hbm.at[pl.ds(row, GRAN), :], dst, sem.at[sl])

    def start_tile(tile, sl):
        def body(g, c):
            copy(tile, g, sl).start()
            return c
        lax.fori_loop(0, ngtot_s[tile], body, 0)

    @pl.when(i == 0)
    def _():
        buf[...] = jnp.zeros_like(buf)
        start_tile(0, 0)

    @pl.when(i < last)
    def _():
        start_tile(i + 1, 1 - slot)

    def wait(g, c):
        copy(i, g, slot).wait()
        return c
    lax.fori_loop(0, ngtot_s[i], wait, 0)

    r_iota, d1, d2, w1, w2 = _sorted_slots(rinfo_ref, locoff_ref, upper_ref)
    wsel = jnp.where(r_iota == d1, w1, jnp.where(r_iota == d2, w2, 0.0)).astype(BF16)
    moe = lax.dot_general(wsel, buf[slot], (((0,), (0,)), ((), ())),
                          preferred_element_type=F32)
    gate2 = mod_ref[5:6, :]
    xn = _layer_norm_rows(DEEPNORM_ALPHA * x1_ref[...] + (1.0 + gate2) * moe)
    o_ref[...] = xn * lng_ref[...] + lnb_ref[...]


def _combine(plan, ys, rinfo, upper, x1, mod_l, ln_g, ln_b, seq):
    ngtot, gdst, _, _, locoff, _ = plan
    t, d = x1.shape
    tiles_per_batch = seq // MOE_TILE
    grid_spec = pltpu.PrefetchScalarGridSpec(
        num_scalar_prefetch=2,
        grid=(t // MOE_TILE,),
        in_specs=[
            pl.BlockSpec(memory_space=pl.ANY),
            pl.BlockSpec((MOE_TILE, LANES), lambda i, *_: (i, 0)),
            pl.BlockSpec((1, N_EXPERTS, MOE_TILE), lambda i, *_: (i, 0, 0)),
            pl.BlockSpec((MOE_TILE, MOE_TILE), lambda i, *_: (0, 0)),
            pl.BlockSpec((MOE_TILE, d), lambda i, *_: (i, 0)),
            pl.BlockSpec((None, N_MOD, d), lambda i, *_: (i // tiles_per_batch, 0, 0)),
            pl.BlockSpec((1, d), lambda i, *_: (0, 0)),
            pl.BlockSpec((1, d), lambda i, *_: (0, 0)),
        ],
        out_specs=pl.BlockSpec((MOE_TILE, d), lambda i, *_: (i, 0)),
        scratch_shapes=[pltpu.VMEM((2, SORT_ROWS, d), BF16), pltpu.SemaphoreType.DMA((2,))],
    )
    return pl.pallas_call(
        _combine_kernel,
        grid_spec=grid_spec,
        out_shape=jax.ShapeDtypeStruct((t, d), F32),
        compiler_params=pltpu.CompilerParams(
            dimension_semantics=("arbitrary",), vmem_limit_bytes=VMEM_LIMIT),
        name="moe_combine",
    )(ngtot, gdst, ys, rinfo, locoff, upper, x1, mod_l, ln_g, ln_b)


def _pad_lanes(a):
    return jnp.pad(a, ((0, 0), (0, LANES - a.shape[1])))


def kernel(x, c, w_mod, b_mod, w_in, conv_w, conv_b, dt_bias, a_log, d_skip, ssd_norm_w, forget_b,
           w_ssd_o, w_att_o, w_o, ln1_g, ln1_b, w_router_group, b_router_group, w_router_expert,
           b_router_expert, w_gate, w_up, w_down, ln2_g, ln2_b):
    bsz, seq, d = x.shape
    depth = w_mod.shape[0]
    t = bsz * seq
    assert seq % MOE_TILE == 0 and d == SSD_INNER
    tm_proj = min(512, seq)
    ssd_rows = min(1024, seq)
    upper = jnp.asarray(np.triu(np.ones((MOE_TILE, MOE_TILE), np.float32), k=1), BF16)
    tq = min(512, seq)
    tk = min(512, seq)

    mod = _modulation(c, w_mod, b_mod).reshape(depth, bsz, N_MOD, d)
    consts = _ssd_constants()
    placement = _attn_placement()
    in_sizes = (SSD_INNER, SSD_CONV_CH, SSD_HEADS, ATT_INNER, ATT_INNER, ATT_INNER, ATT_HEADS, d, d)
    offs = np.concatenate([[0], np.cumsum(in_sizes)]).tolist()

    x2 = x.reshape(t, d)
    for l in range(depth):
        cols = [w_in[l][:, offs[i]:offs[i + 1]] for i in range(len(in_sizes))]
        wz, wxbc, wdt, wq, wk, wv, wf, wgs, wga = cols
        wq = wq * (math.log2(math.e) * ATT_HEAD_DIM ** -0.5)
        big = [w.astype(BF16) for w in (wz, wxbc, wq, wk, wv, wgs, wga)]
        w_small = _pad_lanes(jnp.concatenate([wdt, wf], axis=1))
        z, xbc, q, k, v, gs, ga, small = _inproj(x2, mod[l], big, w_small, seq, tm_proj)

        hvec = jnp.concatenate([
            _pad_lanes(jnp.concatenate([dt_bias[l], forget_b[l]])[None, :]),
            _pad_lanes(a_log[l][None, :]),
            jnp.zeros((6, LANES), F32)], axis=0)
        dskip_x = jnp.repeat(d_skip[l], SSD_HEAD_DIM)[None, :]
        y, fcum = _ssd(xbc.reshape(bsz, seq, SSD_CONV_CH), small.reshape(bsz, seq, LANES),
                       conv_w[l], conv_b[l][None, :], hvec, dskip_x, consts, ssd_rows)

        o = _attention(q.reshape(bsz, seq, ATT_INNER), k.reshape(bsz, seq, ATT_INNER),
                       v.reshape(bsz, seq, ATT_INNER), fcum, placement, tq, tk)

        w_r = _pad_lanes(jnp.concatenate([w_router_expert[l], w_router_group[l]], axis=1))
        b_r = _pad_lanes(jnp.concatenate([b_router_expert[l], b_router_group[l]])[None, :])
        x1, h2, rinfo, cnt = _mixout(
            y.reshape(t, SSD_INNER), z, o.reshape(t, ATT_INNER), gs, ga, x2, mod[l],
            ssd_norm_w[l][None, :], w_ssd_o[l].astype(BF16), w_att_o[l].astype(BF16),
            w_o[l].astype(BF16), ln1_g[l][None, :], ln1_b[l][None, :], w_r, b_r, seq, MOE_TILE)

        plan = _moe_plan(cnt[:, 0, :N_EXPERTS], t)
        xs = _dispatch(plan, h2, rinfo, upper)
        ys = _experts(plan, xs, w_gate, w_up, w_down, l)
        x2 = _combine(plan, ys, rinfo, upper, x1, mod[l], ln2_g[l][None, :], ln2_b[l][None, :],
                      seq)
    return x2.reshape(bsz, seq, d)
```
